```python
import math
import jax, jax.numpy as jnp
from jax import lax
import numpy as np

D_MODEL = 2048
BATCH = 2
SEQ = 16384
DEPTH = 4

GRID_W = 64
CTX_LEN = 256
N_MIXERS = 4
MIXER_CONV, MIXER_MLA, MIXER_POOL, MIXER_DIFF = 0, 1, 2, 3
EPS = 1e-6
ROPE_THETA = 10000.0
Q_BLOCK = 128
N_MOD = 6
CONV_WIDTH = 31
MLA_HEADS = D_MODEL // 128
MLA_NOPE = 128
MLA_ROPE = 64
MLA_V = 128
MLA_Q_RANK = 3 * D_MODEL // 8
MLA_KV_RANK = D_MODEL // 4
POOL_WINDOWS = (2, 4, 8, 16)
POOL_GROUP = D_MODEL // len(POOL_WINDOWS)
DIFF_HEAD_DIM = 128
DIFF_HEADS = D_MODEL // (2 * DIFF_HEAD_DIM)
N_EXPERTS = 16
EC_FACTOR = 2
EXPERT_FF = D_MODEL // 2

kernel_name = 'hybrid_diffusion_conv_mla_pool_diffattn_ecmoe'


def _rms(x, g):
    xf = x.astype(jnp.float32)
    y = xf * lax.rsqrt(jnp.mean(xf * xf, axis=-1, keepdims=True) + EPS)
    return (y * g.astype(jnp.float32)).astype(x.dtype)


def _layernorm(x, g, b):
    xf = x.astype(jnp.float32)
    mu = jnp.mean(xf, axis=-1, keepdims=True)
    var = jnp.mean(jnp.square(xf - mu), axis=-1, keepdims=True)
    y = (xf - mu) * lax.rsqrt(var + EPS)
    return (y * g.astype(jnp.float32) + b.astype(jnp.float32)).astype(x.dtype)


def _axial_rope_table(rows, rot_dim):
    quarter = rot_dim // 4
    inv = ROPE_THETA ** (-jnp.arange(quarter, dtype=jnp.float32) / quarter)
    row_ang = jnp.arange(rows, dtype=jnp.float32)[:, None, None] * inv
    col_ang = jnp.arange(GRID_W, dtype=jnp.float32)[None, :, None] * inv
    ang = jnp.concatenate([jnp.broadcast_to(row_ang, (rows, GRID_W, quarter)),
                           jnp.broadcast_to(col_ang, (rows, GRID_W, quarter))], axis=-1)
    ang = ang.reshape(rows * GRID_W, 2 * quarter)
    return jnp.cos(ang), jnp.sin(ang)


def _rope(x, cos, sin):
    half = x.shape[-1] // 2
    xf = x.astype(jnp.float32)
    x1, x2 = xf[..., :half], xf[..., half:]
    return jnp.concatenate([x1 * cos - x2 * sin, x1 * sin + x2 * cos], axis=-1).astype(x.dtype)


def _sweep_query_blocks(fn, q):
    b, s = q.shape[0], q.shape[1]
    nb = s // Q_BLOCK
    qb = jnp.moveaxis(q.reshape((b, nb, Q_BLOCK) + q.shape[2:]), 1, 0)
    out = jnp.moveaxis(lax.map(fn, qb), 0, 1)
    return out.reshape((b, s) + out.shape[3:])


def _softmax_attend(q, k, v, scale):
    s = jnp.einsum('bqhd,bkhd->bhqk', q, k).astype(jnp.float32) * scale
    p = jax.nn.softmax(s, axis=-1).astype(v.dtype)
    return jnp.einsum('bhqk,bkhd->bqhd', p, v)


def _diff_attend(q, k, v, lam, scale):
    s = jnp.einsum('bqhnd,bkhnd->nbhqk', q, k).astype(jnp.float32) * scale
    p = jax.nn.softmax(s, axis=-1)
    a = (p[0] - lam * p[1]).astype(v.dtype)
    return jnp.einsum('bhqk,bkhd->bqhd', a, v)


def _conv_module(h, pw1_w, pw1_b, dw_w, dw_b, ln_g, ln_b, pw2_w, pw2_b):
    u = h @ pw1_w + pw1_b
    u = u[..., :D_MODEL] * jax.nn.sigmoid(u[..., D_MODEL:])
    pad = CONV_WIDTH // 2
    u = lax.conv_general_dilated(u, dw_w[:, None, :], window_strides=(1,), padding=[(pad, pad)],
                                 dimension_numbers=('NWC', 'WIO', 'NWC'),
                                 feature_group_count=D_MODEL) + dw_b
    u = jax.nn.silu(_layernorm(u, ln_g, ln_b))
    return u @ pw2_w + pw2_b


def _pool_mixer(h, pool_w, pool_scale):
    b, l, _ = h.shape
    cs = jnp.pad(jnp.cumsum(h.astype(jnp.float32), axis=1), ((0, 0), (1, 0), (0, 0)))
    t = jnp.arange(l)
    parts = []
    for g, w in enumerate(POOL_WINDOWS):
        sl = cs[..., g * POOL_GROUP:(g + 1) * POOL_GROUP]
        lo = jnp.clip(t - w // 2, 0, l)
        hi = jnp.clip(t - w // 2 + w, 0, l)
        mean = (sl[:, hi] - sl[:, lo]) / (hi - lo).astype(jnp.float32)[:, None]
        parts.append(mean.astype(h.dtype) - h[..., g * POOL_GROUP:(g + 1) * POOL_GROUP])
    d = jnp.stack(parts, axis=2)
    y = jnp.einsum('blgi,gio->blgo', d, pool_w).reshape(b, l, D_MODEL)
    return y * pool_scale


def _mla_q(h, w_dq, q_norm, w_uq):
    b, l, _ = h.shape
    cq = _rms(h @ w_dq, q_norm)
    return (cq @ w_uq).reshape(b, l, MLA_HEADS, MLA_NOPE + MLA_ROPE)


def _mla_kv(h, w_dkv, kv_norm, w_ukv, cos=None, sin=None):
    b, l, _ = h.shape
    kv_a = h @ w_dkv
    ckv = _rms(kv_a[..., :MLA_KV_RANK], kv_norm)
    k_rope = kv_a[..., MLA_KV_RANK:]
    if cos is not None:
        k_rope = _rope(k_rope, cos, sin)
    kv = (ckv @ w_ukv).reshape(b, l, MLA_HEADS, MLA_NOPE + MLA_V)
    k = jnp.concatenate([kv[..., :MLA_NOPE],
                         jnp.broadcast_to(k_rope[:, :, None, :], (b, l, MLA_HEADS, MLA_ROPE))], axis=-1)
    return k, kv[..., MLA_NOPE:]


def _mla_mixer(n_l, n_c, need_ctx, cos, sin, w_dq, q_norm, w_uq, w_dkv, kv_norm, w_ukv, w_o):
    b, s, _ = n_l.shape
    scale = (MLA_NOPE + MLA_ROPE) ** -0.5
    k_l, v_l = _mla_kv(n_l, w_dkv, kv_norm, w_ukv, cos, sin)
    k_c, v_c = _mla_kv(n_c, w_dkv, kv_norm, w_ukv)
    k_all = jnp.concatenate([k_l, k_c], axis=1)
    v_all = jnp.concatenate([v_l, v_c], axis=1)
    q_l = _mla_q(n_l, w_dq, q_norm, w_uq)
    q_l = jnp.concatenate([q_l[..., :MLA_NOPE],
                           _rope(q_l[..., MLA_NOPE:], cos[:, None, :], sin[:, None, :])], axis=-1)
    o_l = _sweep_query_blocks(lambda qb: _softmax_attend(qb, k_all, v_all, scale), q_l)
    y_l = o_l.reshape(b, s, MLA_HEADS * MLA_V) @ w_o
    y_c = None
    if need_ctx:
        q_c = _mla_q(n_c, w_dq, q_norm, w_uq)
        o_c = _softmax_attend(q_c, k_c, v_c, scale)
        y_c = o_c.reshape(b, n_c.shape[1], MLA_HEADS * MLA_V) @ w_o
    return y_l, y_c


def _diff_project(h, w_qkv):
    b, l, _ = h.shape
    q, k, v = jnp.split(h @ w_qkv, 3, axis=-1)
    return (q.reshape(b, l, DIFF_HEADS, 2, DIFF_HEAD_DIM),
            k.reshape(b, l, DIFF_HEADS, 2, DIFF_HEAD_DIM),
            v.reshape(b, l, DIFF_HEADS, 2 * DIFF_HEAD_DIM))


def _diff_mixer(n_l, n_c, need_ctx, layer_idx, cos, sin, w_qkv, lq1, lk1, lq2, lk2, subln_g, w_o):
    b, s, _ = n_l.shape
    scale = DIFF_HEAD_DIM ** -0.5
    lam_init = 0.8 - 0.6 * math.exp(-0.3 * layer_idx)
    f32 = jnp.float32
    lam = (jnp.exp(jnp.sum(lq1.astype(f32) * lk1.astype(f32)))
           - jnp.exp(jnp.sum(lq2.astype(f32) * lk2.astype(f32))) + lam_init)
    q_l, k_l, v_l = _diff_project(n_l, w_qkv)
    q_l = _rope(q_l, cos[:, None, None, :], sin[:, None, None, :])
    k_l = _rope(k_l, cos[:, None, None, :], sin[:, None, None, :])
    q_c, k_c, v_c = _diff_project(n_c, w_qkv)
    k_all = jnp.concatenate([k_l, k_c], axis=1)
    v_all = jnp.concatenate([v_l, v_c], axis=1)
    o_l = _sweep_query_blocks(lambda qb: _diff_attend(qb, k_all, v_all, lam, scale), q_l)
    o_l = _rms(o_l, subln_g) * (1.0 - lam_init)
    y_l = o_l.reshape(b, s, DIFF_HEADS * 2 * DIFF_HEAD_DIM) @ w_o
    y_c = None
    if need_ctx:
        o_c = _rms(_diff_attend(q_c, k_c, v_c, lam, scale), subln_g) * (1.0 - lam_init)
        y_c = o_c.reshape(b, n_c.shape[1], DIFF_HEADS * 2 * DIFF_HEAD_DIM) @ w_o
    return y_l, y_c


def _expert_choice_ffn(h, router_w, w_gate, w_up, w_down):
    b, n, _ = h.shape
    cap = max(1, EC_FACTOR * n // N_EXPERTS)
    aff = jax.nn.softmax((h @ router_w).astype(jnp.float32), axis=-1)
    g, idx = lax.top_k(jnp.swapaxes(aff, 1, 2), cap)
    bidx = jnp.arange(b)[:, None, None]
    xs = h[bidx, idx]
    a = jnp.einsum('becd,edf->becf', xs, w_gate)
    u = jnp.einsum('becd,edf->becf', xs, w_up)
    y = jnp.einsum('becf,efd->becd', jax.nn.silu(a) * u, w_down)
    y = (y * g[..., None].astype(y.dtype)).astype(h.dtype)
    return jnp.zeros_like(h).at[bidx, idx].add(y)


def setup_inputs(seed: int = 0) -> dict:
    key = jax.random.key(seed)
    keys = jax.random.split(key, 48)
    ki = iter(range(48))

    def nrm(shape, scale):
        return jax.random.normal(keys[next(ki)], shape, jnp.float32) * scale

    def gain(shape):
        return 1.0 + nrm(shape, 0.02)

    D = D_MODEL
    return {
        'x': nrm((BATCH, SEQ, D), 1.0),
        'c': nrm((BATCH, D), 1.0),
        'ctx': nrm((BATCH, CTX_LEN, D), 1.0),
        'c_ctx': nrm((D,), 1.0),
        'ada_w': nrm((DEPTH, D, N_MOD * D), 0.5 * D ** -0.5),
        'ada_b': nrm((DEPTH, N_MOD * D), 0.02),
        'norm_g': gain((DEPTH, 2, D)),
        'final_g': gain((D,)),
        'conv_pw1_w': nrm((D, 2 * D), D ** -0.5),
        'conv_pw1_b': nrm((2 * D,), 0.02),
        'conv_dw_w': nrm((CONV_WIDTH, D), CONV_WIDTH ** -0.5),
        'conv_dw_b': nrm((D,), 0.02),
        'conv_ln_g': gain((D,)),
        'conv_ln_b': nrm((D,), 0.02),
        'conv_pw2_w': nrm((D, D), D ** -0.5),
        'conv_pw2_b': nrm((D,), 0.02),
        'mla_w_dq': nrm((D, MLA_Q_RANK), D ** -0.5),
        'mla_q_norm': gain((MLA_Q_RANK,)),
        'mla_w_uq': nrm((MLA_Q_RANK, MLA_HEADS * (MLA_NOPE + MLA_ROPE)), MLA_Q_RANK ** -0.5),
        'mla_w_dkv': nrm((D, MLA_KV_RANK + MLA_ROPE), D ** -0.5),
        'mla_kv_norm': gain((MLA_KV_RANK,)),
        'mla_w_ukv': nrm((MLA_KV_RANK, MLA_HEADS * (MLA_NOPE + MLA_V)), MLA_KV_RANK ** -0.5),
        'mla_w_o': nrm((MLA_HEADS * MLA_V, D), (MLA_HEADS * MLA_V) ** -0.5),
        'pool_w': nrm((len(POOL_WINDOWS), POOL_GROUP, POOL_GROUP), POOL_GROUP ** -0.5),
        'pool_scale': gain((D,)),
        'diff_w_qkv': nrm((D, 3 * DIFF_HEADS * 2 * DIFF_HEAD_DIM), D ** -0.5),
        'diff_lq1': nrm((DIFF_HEAD_DIM,), 0.1),
        'diff_lk1': nrm((DIFF_HEAD_DIM,), 0.1),
        'diff_lq2': nrm((DIFF_HEAD_DIM,), 0.1),
        'diff_lk2': nrm((DIFF_HEAD_DIM,), 0.1),
        'diff_subln_g': gain((2 * DIFF_HEAD_DIM,)),
        'diff_w_o': nrm((DIFF_HEADS * 2 * DIFF_HEAD_DIM, D), (DIFF_HEADS * 2 * DIFF_HEAD_DIM) ** -0.5),
        'moe_router': nrm((DEPTH, D, N_EXPERTS), D ** -0.5),
        'moe_w_gate': nrm((DEPTH, N_EXPERTS, D, EXPERT_FF), D ** -0.5),
        'moe_w_up': nrm((DEPTH, N_EXPERTS, D, EXPERT_FF), D ** -0.5),
        'moe_w_down': nrm((DEPTH, N_EXPERTS, EXPERT_FF, D), EXPERT_FF ** -0.5),
    }


def reference(x, c, ctx, c_ctx, ada_w, ada_b, norm_g, final_g,
              conv_pw1_w, conv_pw1_b, conv_dw_w, conv_dw_b, conv_ln_g, conv_ln_b, conv_pw2_w, conv_pw2_b,
              mla_w_dq, mla_q_norm, mla_w_uq, mla_w_dkv, mla_kv_norm, mla_w_ukv, mla_w_o,
              pool_w, pool_scale,
              diff_w_qkv, diff_lq1, diff_lk1, diff_lq2, diff_lk2, diff_subln_g, diff_w_o,
              moe_router, moe_w_gate, moe_w_up, moe_w_down):
    b, s, _ = x.shape
    rows = s // GRID_W
    cos_mla, sin_mla = _axial_rope_table(rows, MLA_ROPE)
    cos_diff, sin_diff = _axial_rope_table(rows, DIFF_HEAD_DIM)
    silu_c = jax.nn.silu(c)
    silu_cc = jax.nn.silu(c_ctx)
    x_lat, x_ctx = x, ctx
    for i in range(DEPTH):
        kind = i % N_MIXERS
        need_ctx_out = i < DEPTH - 1
        ctx_needed = need_ctx_out or kind in (MIXER_MLA, MIXER_DIFF)
        mod_l = jnp.split((silu_c @ ada_w[i] + ada_b[i])[:, None, :], N_MOD, axis=-1)
        n_l = _rms(x_lat, norm_g[i, 0]) * (1 + mod_l[1]) + mod_l[0]
        n_c, mod_c = None, None
        if ctx_needed:
            mod_c = jnp.split((silu_cc @ ada_w[i] + ada_b[i])[None, None, :], N_MOD, axis=-1)
            n_c = _rms(x_ctx, norm_g[i, 0]) * (1 + mod_c[1]) + mod_c[0]
        y_c = None
        if kind == MIXER_CONV:
            conv_args = (conv_pw1_w, conv_pw1_b, conv_dw_w, conv_dw_b, conv_ln_g, conv_ln_b, conv_pw2_w, conv_pw2_b)
            y_l = _conv_module(n_l, *conv_args)
            if need_ctx_out:
                y_c = _conv_module(n_c, *conv_args)
        elif kind == MIXER_MLA:
            y_l, y_c = _mla_mixer(n_l, n_c, need_ctx_out, cos_mla, sin_mla, mla_w_dq, mla_q_norm, mla_w_uq,
                                  mla_w_dkv, mla_kv_norm, mla_w_ukv, mla_w_o)
        elif kind == MIXER_POOL:
            y_l = _pool_mixer(n_l, pool_w, pool_scale)
            if need_ctx_out:
                y_c = _pool_mixer(n_c, pool_w, pool_scale)
        else:
            y_l, y_c = _diff_mixer(n_l, n_c, need_ctx_out, i, cos_diff, sin_diff, diff_w_qkv,
                                   diff_lq1, diff_lk1, diff_lq2, diff_lk2, diff_subln_g, diff_w_o)
        x_lat = x_lat + mod_l[2] * y_l
        m_l = _rms(x_lat, norm_g[i, 1]) * (1 + mod_l[4]) + mod_l[3]
        x_lat = x_lat + mod_l[5] * _expert_choice_ffn(m_l, moe_router[i], moe_w_gate[i], moe_w_up[i], moe_w_down[i])
        if need_ctx_out:
            x_ctx = x_ctx + mod_c[2] * y_c
            m_c = _rms(x_ctx, norm_g[i, 1]) * (1 + mod_c[4]) + mod_c[3]
            x_ctx = x_ctx + mod_c[5] * _expert_choice_ffn(m_c, moe_router[i], moe_w_gate[i], moe_w_up[i], moe_w_down[i])
    return _rms(x_lat, final_g)
```

```python
import functools
import math

import jax
import jax.numpy as jnp
from jax import lax
from jax.experimental import pallas as pl
from jax.experimental.pallas import tpu as pltpu

F32 = jnp.float32
BF16 = jnp.bfloat16

GRID_W = 64
EPS = 1e-6
ROPE_THETA = 10000.0
MLA_NOPE = 128
MLA_ROPE = 64
MLA_V = 128
MLA_QK_PAD = 256
DIFF_HEAD_DIM = 128
POOL_WINDOWS = (2, 4, 8, 16)
EC_FACTOR = 2
HALO = 16
NEG_BIG = -1e30

VMEM_LIMIT_BYTES = 56 * 1024 * 1024


def _cp(*sem):
    return pltpu.CompilerParams(dimension_semantics=sem, vmem_limit_bytes=VMEM_LIMIT_BYTES)


def _divisor(n, target, mult):
    best = None
    d = mult
    while d <= min(n, target):
        if n % d == 0:
            best = d
        d += mult
    return best if best is not None else n


def _silu(v):
    return v * jax.nn.sigmoid(v)


def _rms_mod(x, g, sc, sh):
    ms = jnp.mean(x * x, axis=-1, keepdims=True)
    return (x * lax.rsqrt(ms + EPS) * g) * (1.0 + sc) + sh


def _mod_kernel(c_ref, w_ref, b_ref, o_ref):
    s = _silu(c_ref[...]).astype(BF16)
    o_ref[0] = jnp.dot(s, w_ref[0].astype(BF16), preferred_element_type=F32) + b_ref[0]


def _mod_call(c8, ada_w, ada_b):
    depth, d, n = ada_w.shape
    tn = _divisor(n, 1024, 128)
    return pl.pallas_call(
        _mod_kernel,
        grid=(depth, n // tn),
        in_specs=[pl.BlockSpec((8, d), lambda l, j: (0, 0)),
                  pl.BlockSpec((1, d, tn), lambda l, j: (l, 0, j)),
                  pl.BlockSpec((1, 1, tn), lambda l, j: (l, 0, j))],
        out_specs=pl.BlockSpec((1, 8, tn), lambda l, j: (l, 0, j)),
        out_shape=jax.ShapeDtypeStruct((depth, 8, n), F32),
        compiler_params=_cp("arbitrary", "arbitrary"),
        name="adaln_mod",
    )(c8, ada_w, ada_b.reshape(depth, 1, n))


class _Stream:
    def __init__(self, b, s, ctx, d, tm):
        assert s % tm == 0 and ctx <= tm and ctx % HALO == 0 and tm % HALO == 0
        self.b, self.s, self.ctx, self.d, self.tm = b, s, ctx, d, tm
        self.l = s + ctx
        self.n_lat = s // tm
        self.nt = self.n_lat + 1

    def row_spec(self, width, col=None):
        if col is None:
            return pl.BlockSpec((1, self.tm, width), lambda b, i, *_: (b, i, 0))
        return pl.BlockSpec((1, self.tm, width), lambda b, i, *_: (b, i, col))

    def mod_spec(self):
        n_lat = self.n_lat
        return pl.BlockSpec((1, 1, self.d), lambda b, i, *_: (b * 2 + i // n_lat, 0, 0))


def _vec_spec(width):
    return pl.BlockSpec((1, width), lambda *_: (0, 0))


def _norm_kernel(x_ref, g_ref, sc_ref, sh_ref, o_ref):
    o_ref[0] = _rms_mod(x_ref[0], g_ref[...], sc_ref[0], sh_ref[0]).astype(o_ref.dtype)


def _norm_call(st, x, g, sc, sh):
    return pl.pallas_call(
        _norm_kernel,
        grid=(st.b, st.nt),
        in_specs=[st.row_spec(st.d), _vec_spec(st.d), st.mod_spec(), st.mod_spec()],
        out_specs=st.row_spec(st.d),
        out_shape=jax.ShapeDtypeStruct((st.b, st.l, st.d), BF16),
        compiler_params=_cp("parallel", "parallel"),
        name="norm_mod",
    )(x, g.reshape(1, -1), sc, sh)


def _norm_router_kernel(x_ref, g_ref, sc_ref, sh_ref, rw_ref, m_ref, aff_ref):
    m = _rms_mod(x_ref[0], g_ref[...], sc_ref[0], sh_ref[0]).astype(BF16)
    m_ref[0] = m
    logits = lax.dot_general(rw_ref[...], m, (((1,), (1,)), ((), ())), preferred_element_type=F32)
    z = logits - jnp.max(logits, axis=0, keepdims=True)
    e = jnp.exp(z)
    aff_ref[0] = e / jnp.sum(e, axis=0, keepdims=True)


def _norm_router_call(st, x, g, sc, sh, router_t):
    n_e = router_t.shape[0]
    return pl.pallas_call(
        _norm_router_kernel,
        grid=(st.b, st.nt),
        in_specs=[st.row_spec(st.d), _vec_spec(st.d), st.mod_spec(), st.mod_spec(),
                  pl.BlockSpec((n_e, st.d), lambda b, i: (0, 0))],
        out_specs=[st.row_spec(st.d), pl.BlockSpec((1, n_e, st.tm), lambda b, i: (b, 0, i))],
        out_shape=[jax.ShapeDtypeStruct((st.b, st.l, st.d), BF16),
                   jax.ShapeDtypeStruct((st.b, n_e, st.l), F32)],
        compiler_params=_cp("parallel", "parallel"),
        name="moe_norm_router",
    )(x, g.reshape(1, -1), sc, sh, router_t)


def _final_norm_kernel(x_ref, g_ref, o_ref):
    x = x_ref[0]
    ms = jnp.mean(x * x, axis=-1, keepdims=True)
    o_ref[0] = x * lax.rsqrt(ms + EPS) * g_ref[...]


def _final_norm_call(st, x, g):
    return pl.pallas_call(
        _final_norm_kernel,
        grid=(st.b, st.n_lat),
        in_specs=[st.row_spec(st.d), _vec_spec(st.d)],
        out_specs=st.row_spec(st.d),
        out_shape=jax.ShapeDtypeStruct((st.b, st.s, st.d), F32),
        compiler_params=_cp("parallel", "parallel"),
        name="final_norm",
    )(x, g.reshape(1, -1))


def _proj_res_kernel(a_ref, w_ref, bias_ref, x_ref, gate_ref, o_ref):
    y = jnp.dot(a_ref[0], w_ref[...], preferred_element_type=F32) + bias_ref[...]
    o_ref[0] = x_ref[0] + gate_ref[0] * y


def _proj_res_call(st, a, w, bias, x, gate):
    k, n = w.shape
    return pl.pallas_call(
        _proj_res_kernel,
        grid=(st.b, st.nt),
        in_specs=[st.row_spec(k), pl.BlockSpec((k, n), lambda b, i: (0, 0)), _vec_spec(n),
                  st.row_spec(n), st.mod_spec()],
        out_specs=st.row_spec(n),
        out_shape=jax.ShapeDtypeStruct((st.b, st.l, n), F32),
        compiler_params=_cp("parallel", "parallel"),
        name="proj_residual",
    )(a, w, bias.reshape(1, -1), x, gate)


def _glu_kernel(a_ref, wa_ref, wb_ref, ba_ref, bb_ref, o_ref):
    a = a_ref[0]
    u = jnp.dot(a, wa_ref[...], preferred_element_type=F32) + ba_ref[...]
    v = jnp.dot(a, wb_ref[...], preferred_element_type=F32) + bb_ref[...]
    o_ref[0] = u * jax.nn.sigmoid(v)


def _glu_call(st, a, w, bias):
    k, n2 = w.shape
    n = n2 // 2
    tn = _divisor(n, 512, 128)
    nj = n // tn
    b2 = bias.reshape(1, -1)
    return pl.pallas_call(
        _glu_kernel,
        grid=(st.b, st.nt, nj),
        in_specs=[st.row_spec(k),
                  pl.BlockSpec((k, tn), lambda b, i, j: (0, j)),
                  pl.BlockSpec((k, tn), lambda b, i, j: (0, j + nj)),
                  pl.BlockSpec((1, tn), lambda b, i, j: (0, j)),
                  pl.BlockSpec((1, tn), lambda b, i, j: (0, j + nj))],
        out_specs=pl.BlockSpec((1, st.tm, tn), lambda b, i, j: (b, i, j)),
        out_shape=jax.ShapeDtypeStruct((st.b, st.l, n), F32),
        compiler_params=_cp("parallel", "parallel", "arbitrary"),
        name="conv_pw1_glu",
    )(a, w, w, b2, b2)


def _halo_specs(st):
    r = st.tm // HALO
    last = st.l // HALO - 1
    prev = pl.BlockSpec((1, HALO, st.d), lambda b, i: (b, jnp.maximum(i * r - 1, 0), 0))
    nxt = pl.BlockSpec((1, HALO, st.d), lambda b, i: (b, jnp.minimum((i + 1) * r, last), 0))
    return prev, nxt


def _segment_masks(st, i):
    is_ctx = i == st.n_lat
    seg_len = jnp.where(is_ctx, st.ctx, st.s)
    base = jnp.where(is_ctx, 0, i * st.tm)
    rows = lax.broadcasted_iota(jnp.int32, (st.tm, 1), 0)
    pos = base + rows
    valid = pos < seg_len
    prev_ok = jnp.logical_and(i != 0, jnp.logical_not(is_ctx))
    next_ok = i < st.n_lat - 1
    return valid, prev_ok, next_ok, pos, seg_len


def _dwconv_kernel(st, width, rc, cc, u_ref, up_ref, un_ref, w_ref, b_ref, g_ref, beta_ref, o_ref,
                   win_ref, sh_ref, acc_ref):
    i = pl.program_id(1)
    valid, prev_ok, next_ok, _, _ = _segment_masks(st, i)
    tm, d = st.tm, st.d
    pad = width // 2
    win_ref[0:HALO, :] = jnp.where(prev_ok, up_ref[0], 0.0)
    win_ref[HALO:HALO + tm, :] = jnp.where(valid, u_ref[0], 0.0)
    win_ref[HALO + tm:, :] = jnp.where(next_ok, un_ref[0], 0.0)
    n_sh = sh_ref.shape[1]

    for c0 in range(0, d, cc):
        for r in range(1, 8):
            sh_ref[r - 1] = win_ref[r:r + n_sh, c0:c0 + cc]

        def row_chunk(rb, carry, c0=c0):
            r0 = pl.multiple_of(rb * rc, rc)
            acc = jnp.zeros((rc, cc), F32)
            for k in range(width):
                off = HALO - pad + k
                a8, res = 8 * (off // 8), off % 8
                if res == 0:
                    tap = win_ref[pl.ds(r0 + a8, rc), c0:c0 + cc]
                else:
                    tap = sh_ref[res - 1, pl.ds(r0 + a8, rc), :]
                acc = acc + tap * w_ref[k:k + 1, c0:c0 + cc]
            acc_ref[pl.ds(r0, rc), c0:c0 + cc] = acc + b_ref[:, c0:c0 + cc]
            return carry

        lax.fori_loop(0, tm // rc, row_chunk, 0)
    y = acc_ref[...]
    mu = jnp.mean(y, axis=-1, keepdims=True)
    yc = y - mu
    var = jnp.mean(yc * yc, axis=-1, keepdims=True)
    z = yc * lax.rsqrt(var + EPS) * g_ref[...] + beta_ref[...]
    o_ref[0] = _silu(z).astype(o_ref.dtype)


def _dwconv_call(st, u, dw_w, dw_b, ln_g, ln_b):
    width = dw_w.shape[0]
    assert width // 2 < HALO
    prev, nxt = _halo_specs(st)
    rc = 32
    cc = _divisor(st.d, 256, 128)
    n_sh = st.tm + 2 * HALO - 8
    kern = functools.partial(_dwconv_kernel, st, width, rc, cc)
    return pl.pallas_call(
        kern,
        grid=(st.b, st.nt),
        in_specs=[st.row_spec(st.d), prev, nxt,
                  pl.BlockSpec((width, st.d), lambda b, i: (0, 0)),
                  _vec_spec(st.d), _vec_spec(st.d), _vec_spec(st.d)],
        out_specs=st.row_spec(st.d),
        out_shape=jax.ShapeDtypeStruct((st.b, st.l, st.d), BF16),
        scratch_shapes=[pltpu.VMEM((st.tm + 2 * HALO, st.d), F32), pltpu.VMEM((7, n_sh, cc), F32),
                        pltpu.VMEM((st.tm, st.d), F32)],
        compiler_params=_cp("parallel", "parallel"),
        name="dwconv_ln_silu",
    )(u, u, u, dw_w, dw_b.reshape(1, -1), ln_g.reshape(1, -1), ln_b.reshape(1, -1))


def _pool_kernel(st, x_ref, xp_ref, xn_ref, g_ref, sc_ref, sh_ref, w_ref, ps_ref, gate_ref, o_ref, win_ref):
    i = pl.program_id(1)
    valid, prev_ok, next_ok, pos, seg_len = _segment_masks(st, i)
    tm, d = st.tm, st.d
    g, sc, sh = g_ref[...], sc_ref[0], sh_ref[0]
    x = x_ref[0]
    n = jnp.where(valid, _rms_mod(x, g, sc, sh), 0.0)
    win_ref[0:HALO, :] = jnp.where(prev_ok, _rms_mod(xp_ref[0], g, sc, sh), 0.0)
    win_ref[HALO:HALO + tm, :] = n
    win_ref[HALO + tm:, :] = jnp.where(next_ok, _rms_mod(xn_ref[0], g, sc, sh), 0.0)
    grp = d // len(POOL_WINDOWS)
    for gi, w in enumerate(POOL_WINDOWS):
        c0 = gi * grp
        tot = jnp.zeros((tm, grp), F32)
        for j in range(-(w // 2), w - w // 2):
            tot = tot + win_ref[HALO + j:HALO + j + tm, c0:c0 + grp]
        cnt = jnp.minimum(pos - w // 2 + w, seg_len) - jnp.maximum(pos - w // 2, 0)
        cnt = jnp.maximum(cnt, 1).astype(F32)
        diff = (tot / cnt - n[:, c0:c0 + grp]).astype(BF16)
        y = jnp.dot(diff, w_ref[gi], preferred_element_type=F32) * ps_ref[:, c0:c0 + grp]
        o_ref[0, :, c0:c0 + grp] = x[:, c0:c0 + grp] + gate_ref[0][:, c0:c0 + grp] * y


def _pool_call(st, x, g, sc, sh, pool_w, pool_scale, gate):
    prev, nxt = _halo_specs(st)
    ng, grp, _ = pool_w.shape
    return pl.pallas_call(
        functools.partial(_pool_kernel, st),
        grid=(st.b, st.nt),
        in_specs=[st.row_spec(st.d), prev, nxt, _vec_spec(st.d), st.mod_spec(), st.mod_spec(),
                  pl.BlockSpec((ng, grp, grp), lambda b, i: (0, 0, 0)), _vec_spec(st.d), st.mod_spec()],
        out_specs=st.row_spec(st.d),
        out_shape=jax.ShapeDtypeStruct((st.b, st.l, st.d), F32),
        scratch_shapes=[pltpu.VMEM((st.tm + 2 * HALO, st.d), F32)],
        compiler_params=_cp("parallel", "parallel"),
        name="pool_mixer",
    )(x, x, x, g.reshape(1, -1), sc, sh, pool_w, pool_scale.reshape(1, -1), gate)


def _rope_tables(st, rot_dim):
    quarter = rot_dim // 4
    rows = st.s // GRID_W
    inv = ROPE_THETA ** (-jnp.arange(quarter, dtype=F32) / quarter)
    row_ang = jnp.arange(rows, dtype=F32)[:, None, None] * inv
    col_ang = jnp.arange(GRID_W, dtype=F32)[None, :, None] * inv
    ang = jnp.concatenate([jnp.broadcast_to(row_ang, (rows, GRID_W, quarter)),
                           jnp.broadcast_to(col_ang, (rows, GRID_W, quarter))], axis=-1).reshape(st.s, 2 * quarter)
    cos, sin = jnp.cos(ang), jnp.sin(ang)
    c_tab = jnp.concatenate([cos, cos], axis=-1)
    s_tab = jnp.concatenate([-sin, sin], axis=-1)
    c_tab = jnp.concatenate([c_tab, jnp.ones((st.ctx, rot_dim), F32)], axis=0)
    s_tab = jnp.concatenate([s_tab, jnp.zeros((st.ctx, rot_dim), F32)], axis=0)
    return c_tab, s_tab


def _tab_spec(st, width):
    return pl.BlockSpec((st.tm, width), lambda b, i, *_: (i, 0))


def _folded_rope(r, cs):
    t = r * cs
    t = t + pltpu.roll(t, 64, axis=1)
    lane = lax.broadcasted_iota(jnp.int32, t.shape, 1)
    return jnp.where(lane < MLA_ROPE, t, 0.0)


def _mla_down_kernel(q_rank, kv_rank, n_ref, w_ref, qn_ref, kvn_ref, cs_ref, cq_ref, ckv_ref, kr_ref):
    acc = jnp.dot(n_ref[0], w_ref[...], preferred_element_type=F32)
    cq = acc[:, :q_rank]
    cq_ref[0] = (cq * lax.rsqrt(jnp.mean(cq * cq, axis=-1, keepdims=True) + EPS) * qn_ref[...]).astype(BF16)
    ckv = acc[:, q_rank:q_rank + kv_rank]
    ckv_ref[0] = (ckv * lax.rsqrt(jnp.mean(ckv * ckv, axis=-1, keepdims=True) + EPS) * kvn_ref[...]).astype(BF16)
    kr_ref[0] = _folded_rope(acc[:, q_rank + kv_rank:], cs_ref[...]).astype(BF16)


def _mla_down_call(st, n, w_cat, q_norm, kv_norm, cs_tab):
    k, ncols = w_cat.shape
    q_rank, kv_rank = q_norm.shape[0], kv_norm.shape[0]
    assert ncols == q_rank + kv_rank + 128 and q_rank % 128 == 0 and kv_rank % 128 == 0
    return pl.pallas_call(
        functools.partial(_mla_down_kernel, q_rank, kv_rank),
        grid=(st.b, st.nt),
        in_specs=[st.row_spec(k), pl.BlockSpec((k, ncols), lambda b, i: (0, 0)),
                  _vec_spec(q_rank), _vec_spec(kv_rank), _tab_spec(st, 128)],
        out_specs=[st.row_spec(q_rank), st.row_spec(kv_rank), st.row_spec(128)],
        out_shape=[jax.ShapeDtypeStruct((st.b, st.l, q_rank), BF16),
                   jax.ShapeDtypeStruct((st.b, st.l, kv_rank), BF16),
                   jax.ShapeDtypeStruct((st.b, st.l, 128), BF16)],
        compiler_params=_cp("parallel", "parallel"),
        name="mla_down",
    )(n, w_cat, q_norm.reshape(1, -1), kv_norm.reshape(1, -1), cs_tab)


def _mla_uq_kernel(cq_ref, w_ref, cs_ref, q_ref):
    acc = jnp.dot(cq_ref[0], w_ref[...], preferred_element_type=F32)
    q_ref[0, 0, :, :MLA_NOPE] = acc[:, :MLA_NOPE].astype(BF16)
    q_ref[0, 0, :, MLA_NOPE:] = _folded_rope(acc[:, MLA_NOPE:], cs_ref[...]).astype(BF16)


def _mla_uq_call(st, cq, w_ext, cs_tab, heads):
    k = cq.shape[-1]
    return pl.pallas_call(
        _mla_uq_kernel,
        grid=(st.b, st.nt, heads),
        in_specs=[st.row_spec(k), pl.BlockSpec((k, MLA_QK_PAD), lambda b, i, h: (0, h)), _tab_spec(st, 128)],
        out_specs=pl.BlockSpec((1, 1, st.tm, MLA_QK_PAD), lambda b, i, h: (b, h, i, 0)),
        out_shape=jax.ShapeDtypeStruct((st.b, heads, st.l, MLA_QK_PAD), BF16),
        compiler_params=_cp("parallel", "parallel", "arbitrary"),
        name="mla_up_q",
    )(cq, w_ext, cs_tab)


def _mla_ukv_kernel(ckv_ref, w_ref, kr_ref, k_ref, v_ref):
    acc = jnp.dot(ckv_ref[0], w_ref[...], preferred_element_type=F32)
    k_ref[0, 0, :, :MLA_NOPE] = acc[:, :MLA_NOPE].astype(BF16)
    k_ref[0, 0, :, MLA_NOPE:] = kr_ref[0]
    v_ref[0, 0] = acc[:, MLA_NOPE:].astype(BF16)


def _mla_ukv_call(st, ckv, w_ukv, kr, heads):
    k = ckv.shape[-1]
    hw = MLA_NOPE + MLA_V
    return pl.pallas_call(
        _mla_ukv_kernel,
        grid=(st.b, st.nt, heads),
        in_specs=[st.row_spec(k), pl.BlockSpec((k, hw), lambda b, i, h: (0, h)), st.row_spec(128)],
        out_specs=[pl.BlockSpec((1, 1, st.tm, MLA_QK_PAD), lambda b, i, h: (b, h, i, 0)),
                   pl.BlockSpec((1, 1, st.tm, MLA_V), lambda b, i, h: (b, h, i, 0))],
        out_shape=[jax.ShapeDtypeStruct((st.b, heads, st.l, MLA_QK_PAD), BF16),
                   jax.ShapeDtypeStruct((st.b, heads, st.l, MLA_V), BF16)],
        compiler_params=_cp("parallel", "parallel", "arbitrary"),
        name="mla_up_kv",
    )(ckv, w_ukv, kr)


def _key_mask(st, tq, tk, qi, ki, s):
    col = ki * tk + lax.broadcasted_iota(jnp.int32, (tq, tk), 1)
    return jnp.where(col >= st.s, s, NEG_BIG)


def _softmax_step(s, v, m_ref, l_ref, acc_ref):
    m_prev = m_ref[...]
    m_new = jnp.maximum(m_prev, jnp.max(s, axis=-1, keepdims=True))
    alpha = jnp.exp(m_prev - m_new)
    p = jnp.exp(s - m_new)
    l_ref[...] = alpha * l_ref[...] + jnp.sum(p, axis=-1, keepdims=True)
    acc_ref[...] = alpha * acc_ref[...] + jnp.dot(p.astype(BF16), v, preferred_element_type=F32)
    m_ref[...] = m_new


def _mla_attn_kernel(st, tq, tk, scale, q_ref, k_ref, v_ref, o_ref, m_ref, l_ref, acc_ref):
    qi, ki = pl.program_id(2), pl.program_id(3)
    n_lat_q = st.s // tq

    @pl.when(ki == 0)
    def _():
        m_ref[...] = jnp.full_like(m_ref, NEG_BIG)
        l_ref[...] = jnp.zeros_like(l_ref)
        acc_ref[...] = jnp.zeros_like(acc_ref)

    def scores():
        return lax.dot_general(q_ref[0, 0], k_ref[0, 0], (((1,), (1,)), ((), ())),
                               preferred_element_type=F32) * scale

    @pl.when(qi < n_lat_q)
    def _():
        _softmax_step(scores(), v_ref[0, 0], m_ref, l_ref, acc_ref)

    @pl.when(jnp.logical_and(qi == n_lat_q, (ki + 1) * tk > st.s))
    def _():
        _softmax_step(_key_mask(st, tq, tk, qi, ki, scores()), v_ref[0, 0], m_ref, l_ref, acc_ref)

    @pl.when(ki == pl.num_programs(3) - 1)
    def _():
        o_ref[0] = (acc_ref[...] / l_ref[...]).astype(o_ref.dtype)


def _attn_tiles(st):
    tq = _divisor(st.s, 1024, 128)
    assert st.ctx <= tq
    tk = _divisor(st.l, 1536, 128)
    return tq, tk


def _mla_attn_call(st, q, k, v, heads):
    tq, tk = _attn_tiles(st)
    nq = st.s // tq + 1
    scale = (MLA_NOPE + MLA_ROPE) ** -0.5
    return pl.pallas_call(
        functools.partial(_mla_attn_kernel, st, tq, tk, scale),
        grid=(st.b, heads, nq, st.l // tk),
        in_specs=[pl.BlockSpec((1, 1, tq, MLA_QK_PAD), lambda b, h, i, j: (b, h, i, 0)),
                  pl.BlockSpec((1, 1, tk, MLA_QK_PAD), lambda b, h, i, j: (b, h, j, 0)),
                  pl.BlockSpec((1, 1, tk, MLA_V), lambda b, h, i, j: (b, h, j, 0))],
        out_specs=pl.BlockSpec((1, tq, MLA_V), lambda b, h, i, j: (b, i, h)),
        out_shape=jax.ShapeDtypeStruct((st.b, st.l, heads * MLA_V), BF16),
        scratch_shapes=[pltpu.VMEM((tq, 1), F32), pltpu.VMEM((tq, 1), F32), pltpu.VMEM((tq, MLA_V), F32)],
        compiler_params=_cp("parallel", "parallel", "parallel", "arbitrary"),
        name="mla_attention",
    )(q, k, v)


def _diff_qkv_kernel(n_rope_tiles, n_ref, w_ref, c_ref, s_ref, o_ref):
    j = pl.program_id(2)
    acc = jnp.dot(n_ref[0], w_ref[...], preferred_element_type=F32)

    @pl.when(j < n_rope_tiles)
    def _():
        c, s = c_ref[...], s_ref[...]
        for h0 in (0, DIFF_HEAD_DIM):
            xh = acc[:, h0:h0 + DIFF_HEAD_DIM]
            o_ref[0, :, h0:h0 + DIFF_HEAD_DIM] = (xh * c + pltpu.roll(xh, DIFF_HEAD_DIM // 2, axis=1) * s).astype(BF16)

    @pl.when(j >= n_rope_tiles)
    def _():
        o_ref[0] = acc.astype(BF16)


def _diff_qkv_call(st, n, w_qkv, c_tab, s_tab):
    k, ncols = w_qkv.shape
    tn = 2 * DIFF_HEAD_DIM
    n_rope_tiles = (ncols // 3) * 2 // tn
    return pl.pallas_call(
        functools.partial(_diff_qkv_kernel, n_rope_tiles),
        grid=(st.b, st.nt, ncols // tn),
        in_specs=[st.row_spec(k), pl.BlockSpec((k, tn), lambda b, i, j: (0, j)),
                  _tab_spec(st, DIFF_HEAD_DIM), _tab_spec(st, DIFF_HEAD_DIM)],
        out_specs=pl.BlockSpec((1, st.tm, tn), lambda b, i, j: (b, i, j)),
        out_shape=jax.ShapeDtypeStruct((st.b, st.l, ncols), BF16),
        compiler_params=_cp("parallel", "parallel", "arbitrary"),
        name="diff_qkv",
    )(n, w_qkv, c_tab, s_tab)


def _diff_attn_kernel(st, tq, tk, scale, lam_init, q_ref, k_ref, v_ref, lq1_ref, lk1_ref, lq2_ref, lk2_ref, sg_ref,
                      o_ref, m1_ref, l1_ref, a1_ref, m2_ref, l2_ref, a2_ref):
    qi, ki = pl.program_id(2), pl.program_id(3)
    n_lat_q = st.s // tq
    hd = DIFF_HEAD_DIM

    @pl.when(ki == 0)
    def _():
        for m_ref, l_ref, a_ref in ((m1_ref, l1_ref, a1_ref), (m2_ref, l2_ref, a2_ref)):
            m_ref[...] = jnp.full_like(m_ref, NEG_BIG)
            l_ref[...] = jnp.zeros_like(l_ref)
            a_ref[...] = jnp.zeros_like(a_ref)

    def step(masked):
        v = v_ref[0]
        for n, (m_ref, l_ref, a_ref) in enumerate(((m1_ref, l1_ref, a1_ref), (m2_ref, l2_ref, a2_ref))):
            s = lax.dot_general(q_ref[0, :, n * hd:(n + 1) * hd], k_ref[0, :, n * hd:(n + 1) * hd],
                                (((1,), (1,)), ((), ())), preferred_element_type=F32) * scale
            if masked:
                s = _key_mask(st, tq, tk, qi, ki, s)
            _softmax_step(s, v, m_ref, l_ref, a_ref)

    @pl.when(qi < n_lat_q)
    def _():
        step(False)

    @pl.when(jnp.logical_and(qi == n_lat_q, (ki + 1) * tk > st.s))
    def _():
        step(True)

    @pl.when(ki == pl.num_programs(3) - 1)
    def _():
        lam = (jnp.exp(jnp.sum(lq1_ref[...] * lk1_ref[...], axis=-1, keepdims=True))
               - jnp.exp(jnp.sum(lq2_ref[...] * lk2_ref[...], axis=-1, keepdims=True)) + lam_init)
        o = a1_ref[...] / l1_ref[...] - lam * (a2_ref[...] / l2_ref[...])
        o = o * lax.rsqrt(jnp.mean(o * o, axis=-1, keepdims=True) + EPS) * sg_ref[...]
        o_ref[0] = (o * (1.0 - lam_init)).astype(o_ref.dtype)


def _diff_attn_call(st, qkv, lq1, lk1, lq2, lk2, subln_g, heads, lam_init):
    tq, tk = _attn_tiles(st)
    nq = st.s // tq + 1
    hw = 2 * DIFF_HEAD_DIM
    scale = DIFF_HEAD_DIM ** -0.5
    vec = lambda a: a.reshape(1, -1)
    return pl.pallas_call(
        functools.partial(_diff_attn_kernel, st, tq, tk, scale, lam_init),
        grid=(st.b, heads, nq, st.l // tk),
        in_specs=[pl.BlockSpec((1, tq, hw), lambda b, h, i, j: (b, i, h)),
                  pl.BlockSpec((1, tk, hw), lambda b, h, i, j: (b, j, heads + h)),
                  pl.BlockSpec((1, tk, hw), lambda b, h, i, j: (b, j, 2 * heads + h)),
                  _vec_spec(DIFF_HEAD_DIM), _vec_spec(DIFF_HEAD_DIM), _vec_spec(DIFF_HEAD_DIM),
                  _vec_spec(DIFF_HEAD_DIM), _vec_spec(hw)],
        out_specs=pl.BlockSpec((1, tq, hw), lambda b, h, i, j: (b, i, h)),
        out_shape=jax.ShapeDtypeStruct((st.b, st.l, heads * hw), BF16),
        scratch_shapes=[pltpu.VMEM((tq, 1), F32), pltpu.VMEM((tq, 1), F32), pltpu.VMEM((tq, hw), F32),
                        pltpu.VMEM((tq, 1), F32), pltpu.VMEM((tq, 1), F32), pltpu.VMEM((tq, hw), F32)],
        compiler_params=_cp("parallel", "parallel", "parallel", "arbitrary"),
        name="diff_attention",
    )(qkv, qkv, qkv, vec(lq1), vec(lk1), vec(lq2), vec(lk2), vec(subln_g))


def _expert_kernel(x_ref, wg_ref, wu_ref, wd_ref, g_ref, o_ref):
    x = x_ref[0]
    a = jnp.dot(x, wg_ref[0], preferred_element_type=F32)
    u = jnp.dot(x, wu_ref[0], preferred_element_type=F32)
    h = (_silu(a) * u).astype(BF16)
    o_ref[0] = jnp.dot(h, wd_ref[0], preferred_element_type=F32) * g_ref[0]


def _expert_call(xs, wg, wu, wd, gates):
    n_e, rows, d = xs.shape
    ff = wg.shape[-1]
    tm = _divisor(rows, 512, 16)
    return pl.pallas_call(
        _expert_kernel,
        grid=(n_e, rows // tm),
        in_specs=[pl.BlockSpec((1, tm, d), lambda e, i: (e, i, 0)),
                  pl.BlockSpec((1, d, ff), lambda e, i: (e, 0, 0)),
                  pl.BlockSpec((1, d, ff), lambda e, i: (e, 0, 0)),
                  pl.BlockSpec((1, ff, d), lambda e, i: (e, 0, 0)),
                  pl.BlockSpec((1, tm, 1), lambda e, i: (e, i, 0))],
        out_specs=pl.BlockSpec((1, tm, d), lambda e, i: (e, i, 0)),
        out_shape=jax.ShapeDtypeStruct((n_e, rows, d), F32),
        compiler_params=_cp("parallel", "arbitrary"),
        name="moe_experts",
    )(xs, wg, wu, wd, gates)


def _moe(st, x, g, sc, sh, gate, router_w, wg, wu, wd, with_ctx):
    n_e = router_w.shape[-1]
    m, aff = _norm_router_call(st, x, g, sc, sh, router_w.T.astype(BF16))
    cap_l = max(1, EC_FACTOR * st.s // n_e)
    g_sel, idx = lax.top_k(aff[:, :, :st.s], cap_l)
    if with_ctx:
        cap_c = max(1, EC_FACTOR * st.ctx // n_e)
        g_c, idx_c = lax.top_k(aff[:, :, st.s:], cap_c)
        g_sel = jnp.concatenate([g_sel, g_c], axis=-1)
        idx = jnp.concatenate([idx, idx_c + st.s], axis=-1)
    cap = idx.shape[-1]
    bidx = jnp.arange(st.b)[:, None, None]
    xs = m[bidx, idx]
    xs = jnp.swapaxes(xs, 0, 1).reshape(n_e, st.b * cap, st.d)
    gates = jnp.swapaxes(g_sel, 0, 1).reshape(n_e, st.b * cap, 1)
    y = _expert_call(xs, wg.astype(BF16), wu.astype(BF16), wd.astype(BF16), gates)
    y = jnp.swapaxes(y.reshape(n_e, st.b, cap, st.d), 0, 1)
    routed = jnp.zeros((st.b, st.l, st.d), F32).at[bidx, idx].add(y)
    gate_rows = jnp.concatenate([jnp.broadcast_to(gate[0::2], (st.b, st.s, st.d)),
                                 jnp.broadcast_to(gate[1::2], (st.b, st.ctx, st.d))], axis=1)
    return x + gate_rows * routed


def _swap_halves_cols(w, width):
    k, n = w.shape
    w = w.reshape(k, n // width, 2, width // 2)
    return w[:, :, ::-1, :].reshape(k, n)


def kernel(x, c, ctx, c_ctx, ada_w, ada_b, norm_g, final_g, conv_pw1_w, conv_pw1_b, conv_dw_w, conv_dw_b, conv_ln_g, conv_ln_b, conv_pw2_w, conv_pw2_b, mla_w_dq, mla_q_norm, mla_w_uq, mla_w_dkv, mla_kv_norm, mla_w_ukv, mla_w_o, pool_w, pool_scale, diff_w_qkv, diff_lq1, diff_lk1, diff_lq2, diff_lk2, diff_subln_g, diff_w_o, moe_router, moe_w_gate, moe_w_up, moe_w_down):
    b, s, d = x.shape
    n_ctx = ctx.shape[1]
    depth = ada_w.shape[0]
    st = _Stream(b, s, n_ctx, d, _divisor(s, 512, max(n_ctx, 128)))
    xs = jnp.concatenate([x, ctx], axis=1)

    c8 = jnp.zeros((8, d), F32).at[:b].set(c).at[b].set(c_ctx)
    mod = _mod_call(c8, ada_w, ada_b)
    order = jnp.stack([jnp.arange(b), jnp.full((b,), b)], axis=1).reshape(-1)
    mod = mod[:, order, :].reshape(depth, 2 * b, 6, d)
    mods = [[mod[i, :, k, :].reshape(2 * b, 1, d) for k in range(6)] for i in range(depth)]

    heads = mla_w_uq.shape[1] // (MLA_NOPE + MLA_ROPE)
    q_rank, kv_rank = mla_q_norm.shape[0], mla_kv_norm.shape[0]
    diff_heads = diff_w_o.shape[0] // (2 * DIFF_HEAD_DIM)
    zeros_d = jnp.zeros((d,), F32)

    for i in range(depth):
        kind = i % 4
        need_ctx_out = i < depth - 1
        sh1, sc1, gt1, sh2, sc2, gt2 = mods[i]
        if kind == 0:
            n = _norm_call(st, xs, norm_g[i, 0], sc1, sh1)
            u = _glu_call(st, n, conv_pw1_w.astype(BF16), conv_pw1_b)
            v = _dwconv_call(st, u, conv_dw_w, conv_dw_b, conv_ln_g, conv_ln_b)
            xs = _proj_res_call(st, v, conv_pw2_w.astype(BF16), conv_pw2_b, xs, gt1)
        elif kind == 1:
            n = _norm_call(st, xs, norm_g[i, 0], sc1, sh1)
            c_tab, s_tab = _rope_tables(st, MLA_ROPE)
            cs_tab = jnp.concatenate([c_tab, s_tab], axis=-1)
            w_kr = mla_w_dkv[:, kv_rank:]
            w_cat = jnp.concatenate([mla_w_dq, mla_w_dkv[:, :kv_rank], w_kr, _swap_halves_cols(w_kr, MLA_ROPE)],
                                    axis=1).astype(BF16)
            cq, ckv, kr = _mla_down_call(st, n, w_cat, mla_q_norm, mla_kv_norm, cs_tab)
            w_uq = mla_w_uq.reshape(q_rank, heads, MLA_NOPE + MLA_ROPE)
            w_uq_rope = w_uq[:, :, MLA_NOPE:].reshape(q_rank, heads * MLA_ROPE)
            w_uq_ext = jnp.concatenate(
                [w_uq, _swap_halves_cols(w_uq_rope, MLA_ROPE).reshape(q_rank, heads, MLA_ROPE)], axis=-1)
            q = _mla_uq_call(st, cq, w_uq_ext.reshape(q_rank, heads * MLA_QK_PAD).astype(BF16), cs_tab, heads)
            k, v = _mla_ukv_call(st, ckv, mla_w_ukv.astype(BF16), kr, heads)
            o = _mla_attn_call(st, q, k, v, heads)
            xs = _proj_res_call(st, o, mla_w_o.astype(BF16), zeros_d, xs, gt1)
        elif kind == 2:
            xs = _pool_call(st, xs, norm_g[i, 0], sc1, sh1, pool_w.astype(BF16), pool_scale, gt1)
        else:
            n = _norm_call(st, xs, norm_g[i, 0], sc1, sh1)
            c_tab, s_tab = _rope_tables(st, DIFF_HEAD_DIM)
            qkv = _diff_qkv_call(st, n, diff_w_qkv.astype(BF16), c_tab, s_tab)
            lam_init = 0.8 - 0.6 * math.exp(-0.3 * i)
            o = _diff_attn_call(st, qkv, diff_lq1, diff_lk1, diff_lq2, diff_lk2, diff_subln_g, diff_heads,
                                lam_init)
            xs = _proj_res_call(st, o, diff_w_o.astype(BF16), zeros_d, xs, gt1)
        xs = _moe(st, xs, norm_g[i, 1], sc2, sh2, gt2, moe_router[i], moe_w_gate[i], moe_w_up[i], moe_w_down[i],
                  need_ctx_out)
    return _final_norm_call(st, xs, final_g)
```

```python
import functools
import math

import jax
import jax.numpy as jnp
from jax import lax
from jax.experimental import pallas as pl
from jax.experimental.pallas import tpu as pltpu

F32 = jnp.float32
BF16 = jnp.bfloat16

GRID_W = 64
EPS = 1e-6
ROPE_THETA = 10000.0
MLA_NOPE = 128
MLA_ROPE = 64
MLA_V = 128
MLA_QK_PAD = 256
DIFF_HEAD_DIM = 128
POOL_WINDOWS = (2, 4, 8, 16)
EC_FACTOR = 2
HALO = 16
NEG_BIG = -1e30
LOG2E = 1.4426950408889634

VMEM_LIMIT_BYTES = 56 * 1024 * 1024


def _cp(*sem):
    return pltpu.CompilerParams(dimension_semantics=sem, vmem_limit_bytes=VMEM_LIMIT_BYTES)


def _divisor(n, target, mult):
    best = None
    d = mult
    while d <= min(n, target):
        if n % d == 0:
            best = d
        d += mult
    return best if best is not None else n


def _silu(v):
    return v * jax.nn.sigmoid(v)


def _rms_mod(x, g, sc, sh):
    ms = jnp.mean(x * x, axis=-1, keepdims=True)
    return (x * lax.rsqrt(ms + EPS) * g) * (1.0 + sc) + sh


def _mod_kernel(c_ref, w_ref, b_ref, o_ref):
    s = _silu(c_ref[...]).astype(BF16)
    o_ref[0] = jnp.dot(s, w_ref[0].astype(BF16), preferred_element_type=F32) + b_ref[0]


def _mod_call(c8, ada_w, ada_b):
    depth, d, n = ada_w.shape
    tn = _divisor(n, 1024, 128)
    return pl.pallas_call(
        _mod_kernel,
        grid=(depth, n // tn),
        in_specs=[pl.BlockSpec((8, d), lambda l, j: (0, 0)),
                  pl.BlockSpec((1, d, tn), lambda l, j: (l, 0, j)),
                  pl.BlockSpec((1, 1, tn), lambda l, j: (l, 0, j))],
        out_specs=pl.BlockSpec((1, 8, tn), lambda l, j: (l, 0, j)),
        out_shape=jax.ShapeDtypeStruct((depth, 8, n), F32),
        compiler_params=_cp("arbitrary", "arbitrary"),
        name="adaln_mod",
    )(c8, ada_w, ada_b.reshape(depth, 1, n))


class _Stream:
    def __init__(self, b, s, ctx, d, tm):
        assert s % tm == 0 and ctx <= tm and ctx % HALO == 0 and tm % HALO == 0
        self.b, self.s, self.ctx, self.d, self.tm = b, s, ctx, d, tm
        self.l = s + ctx
        self.n_lat = s // tm
        self.nt = self.n_lat + 1

    def row_spec(self, width, col=None):
        if col is None:
            return pl.BlockSpec((1, self.tm, width), lambda b, i, *_: (b, i, 0))
        return pl.BlockSpec((1, self.tm, width), lambda b, i, *_: (b, i, col))

    def mod_spec(self):
        n_lat = self.n_lat
        return pl.BlockSpec((1, 1, self.d), lambda b, i, *_: (b * 2 + i // n_lat, 0, 0))


def _vec_spec(width):
    return pl.BlockSpec((1, width), lambda *_: (0, 0))


def _norm_kernel(x_ref, g_ref, sc_ref, sh_ref, o_ref):
    o_ref[0] = _rms_mod(x_ref[0], g_ref[...], sc_ref[0], sh_ref[0]).astype(o_ref.dtype)


def _norm_call(st, x, g, sc, sh):
    return pl.pallas_call(
        _norm_kernel,
        grid=(st.b, st.nt),
        in_specs=[st.row_spec(st.d), _vec_spec(st.d), st.mod_spec(), st.mod_spec()],
        out_specs=st.row_spec(st.d),
        out_shape=jax.ShapeDtypeStruct((st.b, st.l, st.d), BF16),
        compiler_params=_cp("parallel", "parallel"),
        name="norm_mod",
    )(x, g.reshape(1, -1), sc, sh)


def _norm_router_kernel(x_ref, g_ref, sc_ref, sh_ref, rw_ref, m_ref, aff_ref):
    m = _rms_mod(x_ref[0], g_ref[...], sc_ref[0], sh_ref[0]).astype(BF16)
    m_ref[0] = m
    logits = lax.dot_general(rw_ref[...], m, (((1,), (1,)), ((), ())), preferred_element_type=F32)
    z = logits - jnp.max(logits, axis=0, keepdims=True)
    e = jnp.exp(z)
    aff_ref[0] = e / jnp.sum(e, axis=0, keepdims=True)


def _norm_router_call(st, x, g, sc, sh, router_t):
    n_e = router_t.shape[0]
    return pl.pallas_call(
        _norm_router_kernel,
        grid=(st.b, st.nt),
        in_specs=[st.row_spec(st.d), _vec_spec(st.d), st.mod_spec(), st.mod_spec(),
                  pl.BlockSpec((n_e, st.d), lambda b, i: (0, 0))],
        out_specs=[st.row_spec(st.d), pl.BlockSpec((1, n_e, st.tm), lambda b, i: (b, 0, i))],
        out_shape=[jax.ShapeDtypeStruct((st.b, st.l, st.d), BF16),
                   jax.ShapeDtypeStruct((st.b, n_e, st.l), F32)],
        compiler_params=_cp("parallel", "parallel"),
        name="moe_norm_router",
    )(x, g.reshape(1, -1), sc, sh, router_t)


def _final_norm_kernel(x_ref, g_ref, o_ref):
    x = x_ref[0]
    ms = jnp.mean(x * x, axis=-1, keepdims=True)
    o_ref[0] = x * lax.rsqrt(ms + EPS) * g_ref[...]


def _final_norm_call(st, x, g):
    return pl.pallas_call(
        _final_norm_kernel,
        grid=(st.b, st.n_lat),
        in_specs=[st.row_spec(st.d), _vec_spec(st.d)],
        out_specs=st.row_spec(st.d),
        out_shape=jax.ShapeDtypeStruct((st.b, st.s, st.d), F32),
        compiler_params=_cp("parallel", "parallel"),
        name="final_norm",
    )(x, g.reshape(1, -1))


def _proj_res_kernel(a_ref, w_ref, bias_ref, x_ref, gate_ref, o_ref):
    y = jnp.dot(a_ref[0], w_ref[...], preferred_element_type=F32) + bias_ref[...]
    o_ref[0] = x_ref[0] + gate_ref[0] * y


def _proj_res_call(st, a, w, bias, x, gate):
    k, n = w.shape
    return pl.pallas_call(
        _proj_res_kernel,
        grid=(st.b, st.nt),
        in_specs=[st.row_spec(k), pl.BlockSpec((k, n), lambda b, i: (0, 0)), _vec_spec(n),
                  st.row_spec(n), st.mod_spec()],
        out_specs=st.row_spec(n),
        out_shape=jax.ShapeDtypeStruct((st.b, st.l, n), F32),
        compiler_params=_cp("parallel", "parallel"),
        name="proj_residual",
    )(a, w, bias.reshape(1, -1), x, gate)


def _glu_kernel(a_ref, wa_ref, wb_ref, ba_ref, bb_ref, o_ref):
    a = a_ref[0]
    u = jnp.dot(a, wa_ref[...], preferred_element_type=F32) + ba_ref[...]
    v = jnp.dot(a, wb_ref[...], preferred_element_type=F32) + bb_ref[...]
    o_ref[0] = u * jax.nn.sigmoid(v)


def _glu_call(st, a, w, bias):
    k, n2 = w.shape
    n = n2 // 2
    tn = _divisor(n, 512, 128)
    nj = n // tn
    b2 = bias.reshape(1, -1)
    return pl.pallas_call(
        _glu_kernel,
        grid=(st.b, st.nt, nj),
        in_specs=[st.row_spec(k),
                  pl.BlockSpec((k, tn), lambda b, i, j: (0, j)),
                  pl.BlockSpec((k, tn), lambda b, i, j: (0, j + nj)),
                  pl.BlockSpec((1, tn), lambda b, i, j: (0, j)),
                  pl.BlockSpec((1, tn), lambda b, i, j: (0, j + nj))],
        out_specs=pl.BlockSpec((1, st.tm, tn), lambda b, i, j: (b, i, j)),
        out_shape=jax.ShapeDtypeStruct((st.b, st.l, n), F32),
        compiler_params=_cp("parallel", "parallel", "arbitrary"),
        name="conv_pw1_glu",
    )(a, w, w, b2, b2)


def _halo_specs(st):
    r = st.tm // HALO
    last = st.l // HALO - 1
    prev = pl.BlockSpec((1, HALO, st.d), lambda b, i: (b, jnp.maximum(i * r - 1, 0), 0))
    nxt = pl.BlockSpec((1, HALO, st.d), lambda b, i: (b, jnp.minimum((i + 1) * r, last), 0))
    return prev, nxt


def _segment_masks(st, i):
    is_ctx = i == st.n_lat
    seg_len = jnp.where(is_ctx, st.ctx, st.s)
    base = jnp.where(is_ctx, 0, i * st.tm)
    rows = lax.broadcasted_iota(jnp.int32, (st.tm, 1), 0)
    pos = base + rows
    valid = pos < seg_len
    prev_ok = jnp.logical_and(i != 0, jnp.logical_not(is_ctx))
    next_ok = i < st.n_lat - 1
    return valid, prev_ok, next_ok, pos, seg_len


def _dwconv_kernel(st, width, rc, cc, u_ref, up_ref, un_ref, w_ref, b_ref, g_ref, beta_ref, o_ref,
                   win_ref, sh_ref, acc_ref):
    i = pl.program_id(1)
    valid, prev_ok, next_ok, _, _ = _segment_masks(st, i)
    tm, d = st.tm, st.d
    pad = width // 2
    win_ref[0:HALO, :] = jnp.where(prev_ok, up_ref[0], 0.0)
    win_ref[HALO:HALO + tm, :] = jnp.where(valid, u_ref[0], 0.0)
    win_ref[HALO + tm:, :] = jnp.where(next_ok, un_ref[0], 0.0)
    n_sh = sh_ref.shape[1]

    for c0 in range(0, d, cc):
        for r in range(1, 8):
            sh_ref[r - 1] = win_ref[r:r + n_sh, c0:c0 + cc]

        def row_chunk(rb, carry, c0=c0):
            r0 = pl.multiple_of(rb * rc, rc)
            acc = jnp.zeros((rc, cc), F32)
            for k in range(width):
                off = HALO - pad + k
                a8, res = 8 * (off // 8), off % 8
                if res == 0:
                    tap = win_ref[pl.ds(r0 + a8, rc), c0:c0 + cc]
                else:
                    tap = sh_ref[res - 1, pl.ds(r0 + a8, rc), :]
                acc = acc + tap * w_ref[k:k + 1, c0:c0 + cc]
            acc_ref[pl.ds(r0, rc), c0:c0 + cc] = acc + b_ref[:, c0:c0 + cc]
            return carry

        lax.fori_loop(0, tm // rc, row_chunk, 0)
    y = acc_ref[...]
    mu = jnp.mean(y, axis=-1, keepdims=True)
    yc = y - mu
    var = jnp.mean(yc * yc, axis=-1, keepdims=True)
    z = yc * lax.rsqrt(var + EPS) * g_ref[...] + beta_ref[...]
    o_ref[0] = _silu(z).astype(o_ref.dtype)


def _dwconv_call(st, u, dw_w, dw_b, ln_g, ln_b):
    width = dw_w.shape[0]
    assert width // 2 < HALO
    prev, nxt = _halo_specs(st)
    rc = 32
    cc = _divisor(st.d, 256, 128)
    n_sh = st.tm + 2 * HALO - 8
    kern = functools.partial(_dwconv_kernel, st, width, rc, cc)
    return pl.pallas_call(
        kern,
        grid=(st.b, st.nt),
        in_specs=[st.row_spec(st.d), prev, nxt,
                  pl.BlockSpec((width, st.d), lambda b, i: (0, 0)),
                  _vec_spec(st.d), _vec_spec(st.d), _vec_spec(st.d)],
        out_specs=st.row_spec(st.d),
        out_shape=jax.ShapeDtypeStruct((st.b, st.l, st.d), BF16),
        scratch_shapes=[pltpu.VMEM((st.tm + 2 * HALO, st.d), F32), pltpu.VMEM((7, n_sh, cc), F32),
                        pltpu.VMEM((st.tm, st.d), F32)],
        compiler_params=_cp("parallel", "parallel"),
        name="dwconv_ln_silu",
    )(u, u, u, dw_w, dw_b.reshape(1, -1), ln_g.reshape(1, -1), ln_b.reshape(1, -1))


def _pool_kernel(st, x_ref, xp_ref, xn_ref, g_ref, sc_ref, sh_ref, w_ref, ps_ref, gate_ref, o_ref, win_ref):
    i = pl.program_id(1)
    valid, prev_ok, next_ok, pos, seg_len = _segment_masks(st, i)
    tm, d = st.tm, st.d
    g, sc, sh = g_ref[...], sc_ref[0], sh_ref[0]
    x = x_ref[0]
    n = jnp.where(valid, _rms_mod(x, g, sc, sh), 0.0)
    win_ref[0:HALO, :] = jnp.where(prev_ok, _rms_mod(xp_ref[0], g, sc, sh), 0.0)
    win_ref[HALO:HALO + tm, :] = n
    win_ref[HALO + tm:, :] = jnp.where(next_ok, _rms_mod(xn_ref[0], g, sc, sh), 0.0)
    grp = d // len(POOL_WINDOWS)
    for gi, w in enumerate(POOL_WINDOWS):
        c0 = gi * grp
        tot = jnp.zeros((tm, grp), F32)
        for j in range(-(w // 2), w - w // 2):
            tot = tot + win_ref[HALO + j:HALO + j + tm, c0:c0 + grp]
        cnt = jnp.minimum(pos - w // 2 + w, seg_len) - jnp.maximum(pos - w // 2, 0)
        cnt = jnp.maximum(cnt, 1).astype(F32)
        diff = (tot / cnt - n[:, c0:c0 + grp]).astype(BF16)
        y = jnp.dot(diff, w_ref[gi], preferred_element_type=F32) * ps_ref[:, c0:c0 + grp]
        o_ref[0, :, c0:c0 + grp] = x[:, c0:c0 + grp] + gate_ref[0][:, c0:c0 + grp] * y


def _pool_call(st, x, g, sc, sh, pool_w, pool_scale, gate):
    prev, nxt = _halo_specs(st)
    ng, grp, _ = pool_w.shape
    return pl.pallas_call(
        functools.partial(_pool_kernel, st),
        grid=(st.b, st.nt),
        in_specs=[st.row_spec(st.d), prev, nxt, _vec_spec(st.d), st.mod_spec(), st.mod_spec(),
                  pl.BlockSpec((ng, grp, grp), lambda b, i: (0, 0, 0)), _vec_spec(st.d), st.mod_spec()],
        out_specs=st.row_spec(st.d),
        out_shape=jax.ShapeDtypeStruct((st.b, st.l, st.d), F32),
        scratch_shapes=[pltpu.VMEM((st.tm + 2 * HALO, st.d), F32)],
        compiler_params=_cp("parallel", "parallel"),
        name="pool_mixer",
    )(x, x, x, g.reshape(1, -1), sc, sh, pool_w, pool_scale.reshape(1, -1), gate)


def _rope_tables(st, rot_dim):
    quarter = rot_dim // 4
    rows = st.s // GRID_W
    inv = ROPE_THETA ** (-jnp.arange(quarter, dtype=F32) / quarter)
    row_ang = jnp.arange(rows, dtype=F32)[:, None, None] * inv
    col_ang = jnp.arange(GRID_W, dtype=F32)[None, :, None] * inv
    ang = jnp.concatenate([jnp.broadcast_to(row_ang, (rows, GRID_W, quarter)),
                           jnp.broadcast_to(col_ang, (rows, GRID_W, quarter))], axis=-1).reshape(st.s, 2 * quarter)
    cos, sin = jnp.cos(ang), jnp.sin(ang)
    c_tab = jnp.concatenate([cos, cos], axis=-1)
    s_tab = jnp.concatenate([-sin, sin], axis=-1)
    c_tab = jnp.concatenate([c_tab, jnp.ones((st.ctx, rot_dim), F32)], axis=0)
    s_tab = jnp.concatenate([s_tab, jnp.zeros((st.ctx, rot_dim), F32)], axis=0)
    return c_tab, s_tab


def _tab_spec(st, width):
    return pl.BlockSpec((st.tm, width), lambda b, i, *_: (i, 0))


def _folded_rope(r, cs):
    t = r * cs
    t = t + pltpu.roll(t, 64, axis=1)
    lane = lax.broadcasted_iota(jnp.int32, t.shape, 1)
    return jnp.where(lane < MLA_ROPE, t, 0.0)


def _mla_down_kernel(q_rank, kv_rank, n_ref, w_ref, qn_ref, kvn_ref, cs_ref, cq_ref, ckv_ref, kr_ref):
    acc = jnp.dot(n_ref[0], w_ref[...], preferred_element_type=F32)
    cq = acc[:, :q_rank]
    cq_ref[0] = (cq * lax.rsqrt(jnp.mean(cq * cq, axis=-1, keepdims=True) + EPS) * qn_ref[...]).astype(BF16)
    ckv = acc[:, q_rank:q_rank + kv_rank]
    ckv_ref[0] = (ckv * lax.rsqrt(jnp.mean(ckv * ckv, axis=-1, keepdims=True) + EPS) * kvn_ref[...]).astype(BF16)
    kr_ref[0] = _folded_rope(acc[:, q_rank + kv_rank:], cs_ref[...]).astype(BF16)


def _mla_down_call(st, n, w_cat, q_norm, kv_norm, cs_tab):
    k, ncols = w_cat.shape
    q_rank, kv_rank = q_norm.shape[0], kv_norm.shape[0]
    assert ncols == q_rank + kv_rank + 128 and q_rank % 128 == 0 and kv_rank % 128 == 0
    return pl.pallas_call(
        functools.partial(_mla_down_kernel, q_rank, kv_rank),
        grid=(st.b, st.nt),
        in_specs=[st.row_spec(k), pl.BlockSpec((k, ncols), lambda b, i: (0, 0)),
                  _vec_spec(q_rank), _vec_spec(kv_rank), _tab_spec(st, 128)],
        out_specs=[st.row_spec(q_rank), st.row_spec(kv_rank), st.row_spec(128)],
        out_shape=[jax.ShapeDtypeStruct((st.b, st.l, q_rank), BF16),
                   jax.ShapeDtypeStruct((st.b, st.l, kv_rank), BF16),
                   jax.ShapeDtypeStruct((st.b, st.l, 128), BF16)],
        compiler_params=_cp("parallel", "parallel"),
        name="mla_down",
    )(n, w_cat, q_norm.reshape(1, -1), kv_norm.reshape(1, -1), cs_tab)


def _mla_uq_kernel(heads, qscale, cq_ref, w_ref, cs_ref, qt_ref):
    cq, cs = cq_ref[0], cs_ref[...]
    for h in range(heads):
        acc = jnp.dot(cq, w_ref[:, h * MLA_QK_PAD:(h + 1) * MLA_QK_PAD], preferred_element_type=F32)
        q = jnp.concatenate([acc[:, :MLA_NOPE], _folded_rope(acc[:, MLA_NOPE:], cs)], axis=1) * qscale
        qt_ref[0, h] = q.T.astype(BF16)


def _mla_uq_call(st, cq, w_ext, cs_tab, heads, qscale):
    k = cq.shape[-1]
    return pl.pallas_call(
        functools.partial(_mla_uq_kernel, heads, qscale),
        grid=(st.b, st.nt),
        in_specs=[st.row_spec(k), pl.BlockSpec((k, heads * MLA_QK_PAD), lambda b, i: (0, 0)), _tab_spec(st, 128)],
        out_specs=pl.BlockSpec((1, heads, MLA_QK_PAD, st.tm), lambda b, i: (b, 0, 0, i)),
        out_shape=jax.ShapeDtypeStruct((st.b, heads, MLA_QK_PAD, st.l), BF16),
        compiler_params=_cp("parallel", "parallel"),
        name="mla_up_q",
    )(cq, w_ext, cs_tab)


def _mla_ukv_kernel(heads, ckv_ref, w_ref, kr_ref, k_ref, vt_ref):
    ckv, kr = ckv_ref[0], kr_ref[0]
    hw = MLA_NOPE + MLA_V
    for h in range(heads):
        acc = jnp.dot(ckv, w_ref[:, h * hw:(h + 1) * hw], preferred_element_type=F32)
        k_ref[0, h, :, :MLA_NOPE] = acc[:, :MLA_NOPE].astype(BF16)
        k_ref[0, h, :, MLA_NOPE:] = kr
        vt_ref[0, h] = acc[:, MLA_NOPE:].T.astype(BF16)


def _mla_ukv_call(st, ckv, w_ukv, kr, heads):
    k = ckv.shape[-1]
    hw = MLA_NOPE + MLA_V
    return pl.pallas_call(
        functools.partial(_mla_ukv_kernel, heads),
        grid=(st.b, st.nt),
        in_specs=[st.row_spec(k), pl.BlockSpec((k, heads * hw), lambda b, i: (0, 0)), st.row_spec(128)],
        out_specs=[pl.BlockSpec((1, heads, st.tm, MLA_QK_PAD), lambda b, i: (b, 0, i, 0)),
                   pl.BlockSpec((1, heads, MLA_V, st.tm), lambda b, i: (b, 0, 0, i))],
        out_shape=[jax.ShapeDtypeStruct((st.b, heads, st.l, MLA_QK_PAD), BF16),
                   jax.ShapeDtypeStruct((st.b, heads, MLA_V, st.l), BF16)],
        compiler_params=_cp("parallel", "parallel"),
        name="mla_up_kv",
    )(ckv, w_ukv, kr)


def _softmax_chunk_t(st, tk, s_ref, j, vt, m_ref, l_ref, acc_ref, masked):
    s = s_ref[...]
    if masked:
        row = j * tk + lax.broadcasted_iota(jnp.int32, s.shape, 0)
        s = jnp.where(row >= st.s, s, NEG_BIG)
    m_prev = m_ref[...]
    m_new = jnp.maximum(m_prev, jnp.max(s, axis=0, keepdims=True))
    alpha = jnp.exp2(m_prev - m_new)
    p = jnp.exp2(s - m_new)
    l_ref[...] = alpha * l_ref[...] + jnp.sum(p, axis=0, keepdims=True)
    acc_ref[...] = alpha * acc_ref[...] + jnp.dot(vt, p.astype(BF16), preferred_element_type=F32)
    m_ref[...] = m_new


def _pipelined_key_loop(st, tq, tk, qi, qk, proc, bufs0, bufs1):
    n_lat_q = st.s // tq
    nch = st.l // tk

    @pl.when(qi < n_lat_q)
    def _():
        qk(0, bufs0)

        def pair(jj, carry):
            qk(2 * jj + 1, bufs1)
            proc(2 * jj, bufs0, False)
            qk(2 * jj + 2, bufs0)
            proc(2 * jj + 1, bufs1, False)
            return carry

        lax.fori_loop(0, (nch - 1) // 2, pair, 0)
        if nch % 2 == 1:
            proc(nch - 1, bufs0, False)
        else:
            qk(nch - 1, bufs1)
            proc(nch - 2, bufs0, False)
            proc(nch - 1, bufs1, False)

    @pl.when(qi == n_lat_q)
    def _():
        for j in range(st.s // tk, nch):
            qk(j, bufs0)
            proc(j, bufs0, True)


def _chunk(j, tk):
    return pl.ds(pl.multiple_of(j * tk, tk), tk)


def _mla_attn_kernel(st, tq, tk, qt_ref, k_ref, vt_ref, o_ref, s0_ref, s1_ref, m_ref, l_ref, acc_ref):
    m_ref[...] = jnp.full_like(m_ref, NEG_BIG)
    l_ref[...] = jnp.zeros_like(l_ref)
    acc_ref[...] = jnp.zeros_like(acc_ref)

    def qk(j, s_ref):
        s_ref[...] = jnp.dot(k_ref[0, 0, _chunk(j, tk), :], qt_ref[0, 0], preferred_element_type=F32)

    def proc(j, s_ref, masked):
        _softmax_chunk_t(st, tk, s_ref, j, vt_ref[0, 0, :, _chunk(j, tk)], m_ref, l_ref, acc_ref, masked)

    _pipelined_key_loop(st, tq, tk, pl.program_id(2), qk, proc, s0_ref, s1_ref)
    o_ref[0] = (acc_ref[...] / l_ref[...]).T.astype(o_ref.dtype)


def _attn_tiles(st, tq_target):
    tq = _divisor(st.s, tq_target, 128)
    assert st.ctx <= tq
    tk = _divisor(st.l, 1536, 256 if st.l % 256 == 0 else 128)
    return tq, tk


def _mla_attn_call(st, qt, k, vt, heads):
    tq, tk = _attn_tiles(st, 1024)
    return pl.pallas_call(
        functools.partial(_mla_attn_kernel, st, tq, tk),
        grid=(st.b, heads, st.s // tq + 1),
        in_specs=[pl.BlockSpec((1, 1, MLA_QK_PAD, tq), lambda b, h, i: (b, h, 0, i)),
                  pl.BlockSpec((1, 1, st.l, MLA_QK_PAD), lambda b, h, i: (b, h, 0, 0)),
                  pl.BlockSpec((1, 1, MLA_V, st.l), lambda b, h, i: (b, h, 0, 0))],
        out_specs=pl.BlockSpec((1, tq, MLA_V), lambda b, h, i: (b, i, h)),
        out_shape=jax.ShapeDtypeStruct((st.b, st.l, heads * MLA_V), BF16),
        scratch_shapes=[pltpu.VMEM((tk, tq), F32), pltpu.VMEM((tk, tq), F32),
                        pltpu.VMEM((1, tq), F32), pltpu.VMEM((1, tq), F32), pltpu.VMEM((MLA_V, tq), F32)],
        compiler_params=_cp("parallel", "parallel", "arbitrary"),
        name="mla_attention",
    )(qt, k, vt)


def _diff_proj_kernel(heads, mode, qscale, n_ref, w_ref, c_ref, s_ref, o_ref):
    hd = DIFF_HEAD_DIM
    acc = jnp.dot(n_ref[0], w_ref[...], preferred_element_type=F32)
    if mode != "v":
        c, s = c_ref[...], s_ref[...]
    for h in range(heads):
        if mode == "v":
            o_ref[0, h] = acc[:, 2 * h * hd:2 * (h + 1) * hd].T.astype(BF16)
            continue
        halves = []
        for n in range(2):
            xh = acc[:, (2 * h + n) * hd:(2 * h + n + 1) * hd]
            halves.append(xh * c + pltpu.roll(xh, hd // 2, axis=1) * s)
        if mode == "k":
            o_ref[0, :, 2 * h * hd:(2 * h + 1) * hd] = halves[0].astype(BF16)
            o_ref[0, :, (2 * h + 1) * hd:2 * (h + 1) * hd] = halves[1].astype(BF16)
        else:
            o_ref[0, h] = (jnp.concatenate(halves, axis=1) * qscale).T.astype(BF16)


def _diff_proj_call(st, n, w_qkv, c_tab, s_tab, heads, mode, qscale=1.0):
    k = w_qkv.shape[0]
    hw = 2 * DIFF_HEAD_DIM
    third = {"q": 0, "k": 1, "v": 2}[mode]
    if mode == "k":
        out_spec = st.row_spec(heads * hw)
        out_shape = jax.ShapeDtypeStruct((st.b, st.l, heads * hw), BF16)
    else:
        out_spec = pl.BlockSpec((1, heads, hw, st.tm), lambda b, i: (b, 0, 0, i))
        out_shape = jax.ShapeDtypeStruct((st.b, heads, hw, st.l), BF16)
    return pl.pallas_call(
        functools.partial(_diff_proj_kernel, heads, mode, qscale),
        grid=(st.b, st.nt),
        in_specs=[st.row_spec(k), pl.BlockSpec((k, heads * hw), lambda b, i: (0, third)),
                  _tab_spec(st, DIFF_HEAD_DIM), _tab_spec(st, DIFF_HEAD_DIM)],
        out_specs=out_spec,
        out_shape=out_shape,
        compiler_params=_cp("parallel", "parallel"),
        name="diff_proj_" + mode,
    )(n, w_qkv, c_tab, s_tab)


def _diff_attn_kernel(st, tq, tk, lam_init, qt_ref, k_ref, vt_ref, lq1_ref, lk1_ref, lq2_ref, lk2_ref, sg_ref, o_ref,
                      sa0, sb0, sa1, sb1, m1, l1, a1, m2, l2, a2):
    hd = DIFF_HEAD_DIM
    for m_ref, l_ref, a_ref in ((m1, l1, a1), (m2, l2, a2)):
        m_ref[...] = jnp.full_like(m_ref, NEG_BIG)
        l_ref[...] = jnp.zeros_like(l_ref)
        a_ref[...] = jnp.zeros_like(a_ref)

    def qk(j, bufs):
        kc = k_ref[0, _chunk(j, tk), :]
        bufs[0][...] = jnp.dot(kc[:, :hd], qt_ref[0, 0, :hd, :], preferred_element_type=F32)
        bufs[1][...] = jnp.dot(kc[:, hd:], qt_ref[0, 0, hd:, :], preferred_element_type=F32)

    def proc(j, bufs, masked):
        vt = vt_ref[0, 0, :, _chunk(j, tk)]
        _softmax_chunk_t(st, tk, bufs[0], j, vt, m1, l1, a1, masked)
        _softmax_chunk_t(st, tk, bufs[1], j, vt, m2, l2, a2, masked)

    _pipelined_key_loop(st, tq, tk, pl.program_id(2), qk, proc, (sa0, sb0), (sa1, sb1))
    lam = (jnp.exp(jnp.sum(lq1_ref[...] * lk1_ref[...], axis=-1, keepdims=True))
           - jnp.exp(jnp.sum(lq2_ref[...] * lk2_ref[...], axis=-1, keepdims=True)) + lam_init)
    o = a1[...] / l1[...] - lam * (a2[...] / l2[...])
    o = o * lax.rsqrt(jnp.mean(o * o, axis=0, keepdims=True) + EPS) * sg_ref[...]
    o_ref[0] = (o * (1.0 - lam_init)).T.astype(o_ref.dtype)


def _diff_attn_call(st, qt, k, vt, lq1, lk1, lq2, lk2, subln_g, heads, lam_init):
    tq, tk = _attn_tiles(st, 512)
    hw = 2 * DIFF_HEAD_DIM
    vec = lambda a: a.reshape(1, -1)
    return pl.pallas_call(
        functools.partial(_diff_attn_kernel, st, tq, tk, lam_init),
        grid=(st.b, heads, st.s // tq + 1),
        in_specs=[pl.BlockSpec((1, 1, hw, tq), lambda b, h, i: (b, h, 0, i)),
                  pl.BlockSpec((1, st.l, hw), lambda b, h, i: (b, 0, h)),
                  pl.BlockSpec((1, 1, hw, st.l), lambda b, h, i: (b, h, 0, 0)),
                  _vec_spec(DIFF_HEAD_DIM), _vec_spec(DIFF_HEAD_DIM), _vec_spec(DIFF_HEAD_DIM),
                  _vec_spec(DIFF_HEAD_DIM), pl.BlockSpec((hw, 1), lambda *_: (0, 0))],
        out_specs=pl.BlockSpec((1, tq, hw), lambda b, h, i: (b, i, h)),
        out_shape=jax.ShapeDtypeStruct((st.b, st.l, heads * hw), BF16),
        scratch_shapes=[pltpu.VMEM((tk, tq), F32)] * 4
        + [pltpu.VMEM((1, tq), F32), pltpu.VMEM((1, tq), F32), pltpu.VMEM((hw, tq), F32)] * 2,
        compiler_params=_cp("parallel", "parallel", "arbitrary"),
        name="diff_attention",
    )(qt, k, vt, vec(lq1), vec(lk1), vec(lq2), vec(lk2), subln_g.reshape(-1, 1))


def _expert_kernel(x_ref, wg_ref, wu_ref, wd_ref, g_ref, o_ref):
    x = x_ref[0]
    a = jnp.dot(x, wg_ref[0], preferred_element_type=F32)
    u = jnp.dot(x, wu_ref[0], preferred_element_type=F32)
    h = (_silu(a) * u).astype(BF16)
    o_ref[0] = jnp.dot(h, wd_ref[0], preferred_element_type=F32) * g_ref[0]


def _expert_call(xs, wg, wu, wd, gates):
    n_e, rows, d = xs.shape
    ff = wg.shape[-1]
    tm = _divisor(rows, 512, 16)
    return pl.pallas_call(
        _expert_kernel,
        grid=(n_e, rows // tm),
        in_specs=[pl.BlockSpec((1, tm, d), lambda e, i: (e, i, 0)),
                  pl.BlockSpec((1, d, ff), lambda e, i: (e, 0, 0)),
                  pl.BlockSpec((1, d, ff), lambda e, i: (e, 0, 0)),
                  pl.BlockSpec((1, ff, d), lambda e, i: (e, 0, 0)),
                  pl.BlockSpec((1, tm, 1), lambda e, i: (e, i, 0))],
        out_specs=pl.BlockSpec((1, tm, d), lambda e, i: (e, i, 0)),
        out_shape=jax.ShapeDtypeStruct((n_e, rows, d), F32),
        compiler_params=_cp("parallel", "arbitrary"),
        name="moe_experts",
    )(xs, wg, wu, wd, gates)


def _moe(st, x, g, sc, sh, gate, router_w, wg, wu, wd, with_ctx):
    n_e = router_w.shape[-1]
    m, aff = _norm_router_call(st, x, g, sc, sh, router_w.T.astype(BF16))
    cap_l = max(1, EC_FACTOR * st.s // n_e)
    g_sel, idx = lax.top_k(aff[:, :, :st.s], cap_l)
    if with_ctx:
        cap_c = max(1, EC_FACTOR * st.ctx // n_e)
        g_c, idx_c = lax.top_k(aff[:, :, st.s:], cap_c)
        g_sel = jnp.concatenate([g_sel, g_c], axis=-1)
        idx = jnp.concatenate([idx, idx_c + st.s], axis=-1)
    cap = idx.shape[-1]
    bidx = jnp.arange(st.b)[:, None, None]
    xs = m[bidx, idx]
    xs = jnp.swapaxes(xs, 0, 1).reshape(n_e, st.b * cap, st.d)
    gates = jnp.swapaxes(g_sel, 0, 1).reshape(n_e, st.b * cap, 1)
    y = _expert_call(xs, wg.astype(BF16), wu.astype(BF16), wd.astype(BF16), gates)
    y = jnp.swapaxes(y.reshape(n_e, st.b, cap, st.d), 0, 1)
    routed = jnp.zeros((st.b, st.l, st.d), F32).at[bidx, idx].add(y)
    gate_rows = jnp.concatenate([jnp.broadcast_to(gate[0::2], (st.b, st.s, st.d)),
                                 jnp.broadcast_to(gate[1::2], (st.b, st.ctx, st.d))], axis=1)
    return x + gate_rows * routed


def _swap_halves_cols(w, width):
    k, n = w.shape
    w = w.reshape(k, n // width, 2, width // 2)
    return w[:, :, ::-1, :].reshape(k, n)


def kernel(x, c, ctx, c_ctx, ada_w, ada_b, norm_g, final_g, conv_pw1_w, conv_pw1_b, conv_dw_w, conv_dw_b, conv_ln_g, conv_ln_b, conv_pw2_w, conv_pw2_b, mla_w_dq, mla_q_norm, mla_w_uq, mla_w_dkv, mla_kv_norm, mla_w_ukv, mla_w_o, pool_w, pool_scale, diff_w_qkv, diff_lq1, diff_lk1, diff_lq2, diff_lk2, diff_subln_g, diff_w_o, moe_router, moe_w_gate, moe_w_up, moe_w_down):
    b, s, d = x.shape
    n_ctx = ctx.shape[1]
    depth = ada_w.shape[0]
    st = _Stream(b, s, n_ctx, d, _divisor(s, 512, max(n_ctx, 128)))
    xs = jnp.concatenate([x, ctx], axis=1)

    c8 = jnp.zeros((8, d), F32).at[:b].set(c).at[b].set(c_ctx)
    mod = _mod_call(c8, ada_w, ada_b)
    order = jnp.stack([jnp.arange(b), jnp.full((b,), b)], axis=1).reshape(-1)
    mod = mod[:, order, :].reshape(depth, 2 * b, 6, d)
    mods = [[mod[i, :, k, :].reshape(2 * b, 1, d) for k in range(6)] for i in range(depth)]

    heads = mla_w_uq.shape[1] // (MLA_NOPE + MLA_ROPE)
    q_rank, kv_rank = mla_q_norm.shape[0], mla_kv_norm.shape[0]
    diff_heads = diff_w_o.shape[0] // (2 * DIFF_HEAD_DIM)
    zeros_d = jnp.zeros((d,), F32)

    for i in range(depth):
        kind = i % 4
        need_ctx_out = i < depth - 1
        sh1, sc1, gt1, sh2, sc2, gt2 = mods[i]
        if kind == 0:
            n = _norm_call(st, xs, norm_g[i, 0], sc1, sh1)
            u = _glu_call(st, n, conv_pw1_w.astype(BF16), conv_pw1_b)
            v = _dwconv_call(st, u, conv_dw_w, conv_dw_b, conv_ln_g, conv_ln_b)
            xs = _proj_res_call(st, v, conv_pw2_w.astype(BF16), conv_pw2_b, xs, gt1)
        elif kind == 1:
            n = _norm_call(st, xs, norm_g[i, 0], sc1, sh1)
            c_tab, s_tab = _rope_tables(st, MLA_ROPE)
            cs_tab = jnp.concatenate([c_tab, s_tab], axis=-1)
            w_kr = mla_w_dkv[:, kv_rank:]
            w_cat = jnp.concatenate([mla_w_dq, mla_w_dkv[:, :kv_rank], w_kr, _swap_halves_cols(w_kr, MLA_ROPE)],
                                    axis=1).astype(BF16)
            cq, ckv, kr = _mla_down_call(st, n, w_cat, mla_q_norm, mla_kv_norm, cs_tab)
            w_uq = mla_w_uq.reshape(q_rank, heads, MLA_NOPE + MLA_ROPE)
            w_uq_rope = w_uq[:, :, MLA_NOPE:].reshape(q_rank, heads * MLA_ROPE)
            w_uq_ext = jnp.concatenate(
                [w_uq, _swap_halves_cols(w_uq_rope, MLA_ROPE).reshape(q_rank, heads, MLA_ROPE)], axis=-1)
            qscale = (MLA_NOPE + MLA_ROPE) ** -0.5 * LOG2E
            qt = _mla_uq_call(st, cq, w_uq_ext.reshape(q_rank, heads * MLA_QK_PAD).astype(BF16), cs_tab, heads, qscale)
            k, vt = _mla_ukv_call(st, ckv, mla_w_ukv.astype(BF16), kr, heads)
            o = _mla_attn_call(st, qt, k, vt, heads)
            xs = _proj_res_call(st, o, mla_w_o.astype(BF16), zeros_d, xs, gt1)
        elif kind == 2:
            xs = _pool_call(st, xs, norm_g[i, 0], sc1, sh1, pool_w.astype(BF16), pool_scale, gt1)
        else:
            n = _norm_call(st, xs, norm_g[i, 0], sc1, sh1)
            c_tab, s_tab = _rope_tables(st, DIFF_HEAD_DIM)
            w_qkv = diff_w_qkv.astype(BF16)
            qt = _diff_proj_call(st, n, w_qkv, c_tab, s_tab, diff_heads, "q", DIFF_HEAD_DIM ** -0.5 * LOG2E)
            k = _diff_proj_call(st, n, w_qkv, c_tab, s_tab, diff_heads, "k")
            vt = _diff_proj_call(st, n, w_qkv, c_tab, s_tab, diff_heads, "v")
            lam_init = 0.8 - 0.6 * math.exp(-0.3 * i)
            o = _diff_attn_call(st, qt, k, vt, diff_lq1, diff_lk1, diff_lq2, diff_lk2, diff_subln_g, diff_heads,
                                lam_init)
            xs = _proj_res_call(st, o, diff_w_o.astype(BF16), zeros_d, xs, gt1)
        xs = _moe(st, xs, norm_g[i, 1], sc2, sh2, gt2, moe_router[i], moe_w_gate[i], moe_w_up[i], moe_w_down[i],
                  need_ctx_out)
    return _final_norm_call(st, xs, final_g)
```

```python
import functools
import math

import jax
import jax.numpy as jnp
from jax import lax
from jax.experimental import pallas as pl
from jax.experimental.pallas import tpu as pltpu

F32 = jnp.float32
BF16 = jnp.bfloat16

GRID_W = 64
EPS = 1e-6
ROPE_THETA = 10000.0
MLA_NOPE = 128
MLA_ROPE = 64
MLA_V = 128
MLA_QK_PAD = 256
DIFF_HEAD_DIM = 128
POOL_WINDOWS = (2, 4, 8, 16)
EC_FACTOR = 2
HALO = 16
NEG_BIG = -1e30
LOG2E = 1.4426950408889634

VMEM_LIMIT_BYTES = 56 * 1024 * 1024


def _cp(*sem):
    return pltpu.CompilerParams(dimension_semantics=sem, vmem_limit_bytes=VMEM_LIMIT_BYTES)


def _divisor(n, target, mult):
    best = None
    d = mult
    while d <= min(n, target):
        if n % d == 0:
            best = d
        d += mult
    return best if best is not None else n


def _silu(v):
    return v * jax.nn.sigmoid(v)


def _rms_mod(x, g, sc, sh):
    ms = jnp.mean(x * x, axis=-1, keepdims=True)
    return (x * lax.rsqrt(ms + EPS) * g) * (1.0 + sc) + sh


def _mod_kernel(c_ref, w_ref, b_ref, o_ref):
    s = _silu(c_ref[...]).astype(BF16)
    o_ref[0] = jnp.dot(s, w_ref[0].astype(BF16), preferred_element_type=F32) + b_ref[0]


def _mod_call(c8, ada_w, ada_b):
    depth, d, n = ada_w.shape
    tn = _divisor(n, 1024, 128)
    return pl.pallas_call(
        _mod_kernel,
        grid=(depth, n // tn),
        in_specs=[pl.BlockSpec((8, d), lambda l, j: (0, 0)),
                  pl.BlockSpec((1, d, tn), lambda l, j: (l, 0, j)),
                  pl.BlockSpec((1, 1, tn), lambda l, j: (l, 0, j))],
        out_specs=pl.BlockSpec((1, 8, tn), lambda l, j: (l, 0, j)),
        out_shape=jax.ShapeDtypeStruct((depth, 8, n), F32),
        compiler_params=_cp("arbitrary", "arbitrary"),
        name="adaln_mod",
    )(c8, ada_w, ada_b.reshape(depth, 1, n))


class _Stream:
    def __init__(self, b, s, ctx, d, tm):
        assert s % tm == 0 and ctx <= tm and ctx % HALO == 0 and tm % HALO == 0
        self.b, self.s, self.ctx, self.d, self.tm = b, s, ctx, d, tm
        self.l = s + ctx
        self.n_lat = s // tm
        self.nt = self.n_lat + 1

    def row_spec(self, width, col=None):
        if col is None:
            return pl.BlockSpec((1, self.tm, width), lambda b, i, *_: (b, i, 0))
        return pl.BlockSpec((1, self.tm, width), lambda b, i, *_: (b, i, col))

    def mod_spec(self):
        n_lat = self.n_lat
        return pl.BlockSpec((1, 1, self.d), lambda b, i, *_: (b * 2 + i // n_lat, 0, 0))


def _vec_spec(width):
    return pl.BlockSpec((1, width), lambda *_: (0, 0))


def _norm_kernel(x_ref, g_ref, sc_ref, sh_ref, o_ref):
    o_ref[0] = _rms_mod(x_ref[0], g_ref[...], sc_ref[0], sh_ref[0]).astype(o_ref.dtype)


def _norm_call(st, x, g, sc, sh):
    return pl.pallas_call(
        _norm_kernel,
        grid=(st.b, st.nt),
        in_specs=[st.row_spec(st.d), _vec_spec(st.d), st.mod_spec(), st.mod_spec()],
        out_specs=st.row_spec(st.d),
        out_shape=jax.ShapeDtypeStruct((st.b, st.l, st.d), BF16),
        compiler_params=_cp("parallel", "parallel"),
        name="norm_mod",
    )(x, g.reshape(1, -1), sc, sh)


def _norm_router_kernel(x_ref, g_ref, sc_ref, sh_ref, rw_ref, m_ref, aff_ref):
    m = _rms_mod(x_ref[0], g_ref[...], sc_ref[0], sh_ref[0]).astype(BF16)
    m_ref[0] = m
    logits = lax.dot_general(rw_ref[...], m, (((1,), (1,)), ((), ())), preferred_element_type=F32)
    z = logits - jnp.max(logits, axis=0, keepdims=True)
    e = jnp.exp(z)
    aff_ref[0] = e / jnp.sum(e, axis=0, keepdims=True)


def _norm_router_call(st, x, g, sc, sh, router_t):
    n_e = router_t.shape[0]
    return pl.pallas_call(
        _norm_router_kernel,
        grid=(st.b, st.nt),
        in_specs=[st.row_spec(st.d), _vec_spec(st.d), st.mod_spec(), st.mod_spec(),
                  pl.BlockSpec((n_e, st.d), lambda b, i: (0, 0))],
        out_specs=[st.row_spec(st.d), pl.BlockSpec((1, n_e, st.tm), lambda b, i: (b, 0, i))],
        out_shape=[jax.ShapeDtypeStruct((st.b, st.l, st.d), BF16),
                   jax.ShapeDtypeStruct((st.b, n_e, st.l), F32)],
        compiler_params=_cp("parallel", "parallel"),
        name="moe_norm_router",
    )(x, g.reshape(1, -1), sc, sh, router_t)


def _final_norm_kernel(x_ref, g_ref, o_ref):
    x = x_ref[0]
    ms = jnp.mean(x * x, axis=-1, keepdims=True)
    o_ref[0] = x * lax.rsqrt(ms + EPS) * g_ref[...]


def _final_norm_call(st, x, g):
    return pl.pallas_call(
        _final_norm_kernel,
        grid=(st.b, st.n_lat),
        in_specs=[st.row_spec(st.d), _vec_spec(st.d)],
        out_specs=st.row_spec(st.d),
        out_shape=jax.ShapeDtypeStruct((st.b, st.s, st.d), F32),
        compiler_params=_cp("parallel", "parallel"),
        name="final_norm",
    )(x, g.reshape(1, -1))


def _proj_res_kernel(a_ref, w_ref, bias_ref, x_ref, gate_ref, o_ref):
    y = jnp.dot(a_ref[0], w_ref[...], preferred_element_type=F32) + bias_ref[...]
    o_ref[0] = x_ref[0] + gate_ref[0] * y


def _proj_res_call(st, a, w, bias, x, gate):
    k, n = w.shape
    return pl.pallas_call(
        _proj_res_kernel,
        grid=(st.b, st.nt),
        in_specs=[st.row_spec(k), pl.BlockSpec((k, n), lambda b, i: (0, 0)), _vec_spec(n),
                  st.row_spec(n), st.mod_spec()],
        out_specs=st.row_spec(n),
        out_shape=jax.ShapeDtypeStruct((st.b, st.l, n), F32),
        compiler_params=_cp("parallel", "parallel"),
        name="proj_residual",
    )(a, w, bias.reshape(1, -1), x, gate)


def _glu_kernel(a_ref, wa_ref, wb_ref, ba_ref, bb_ref, o_ref):
    a = a_ref[0]
    u = jnp.dot(a, wa_ref[...], preferred_element_type=F32) + ba_ref[...]
    v = jnp.dot(a, wb_ref[...], preferred_element_type=F32) + bb_ref[...]
    o_ref[0] = u * jax.nn.sigmoid(v)


def _glu_call(st, a, w, bias):
    k, n2 = w.shape
    n = n2 // 2
    tn = _divisor(n, 512, 128)
    nj = n // tn
    b2 = bias.reshape(1, -1)
    return pl.pallas_call(
        _glu_kernel,
        grid=(st.b, st.nt, nj),
        in_specs=[st.row_spec(k),
                  pl.BlockSpec((k, tn), lambda b, i, j: (0, j)),
                  pl.BlockSpec((k, tn), lambda b, i, j: (0, j + nj)),
                  pl.BlockSpec((1, tn), lambda b, i, j: (0, j)),
                  pl.BlockSpec((1, tn), lambda b, i, j: (0, j + nj))],
        out_specs=pl.BlockSpec((1, st.tm, tn), lambda b, i, j: (b, i, j)),
        out_shape=jax.ShapeDtypeStruct((st.b, st.l, n), F32),
        compiler_params=_cp("parallel", "parallel", "arbitrary"),
        name="conv_pw1_glu",
    )(a, w, w, b2, b2)


def _halo_specs(st):
    r = st.tm // HALO
    last = st.l // HALO - 1
    prev = pl.BlockSpec((1, HALO, st.d), lambda b, i: (b, jnp.maximum(i * r - 1, 0), 0))
    nxt = pl.BlockSpec((1, HALO, st.d), lambda b, i: (b, jnp.minimum((i + 1) * r, last), 0))
    return prev, nxt


def _segment_masks(st, i):
    is_ctx = i == st.n_lat
    seg_len = jnp.where(is_ctx, st.ctx, st.s)
    base = jnp.where(is_ctx, 0, i * st.tm)
    rows = lax.broadcasted_iota(jnp.int32, (st.tm, 1), 0)
    pos = base + rows
    valid = pos < seg_len
    prev_ok = jnp.logical_and(i != 0, jnp.logical_not(is_ctx))
    next_ok = i < st.n_lat - 1
    return valid, prev_ok, next_ok, pos, seg_len


def _dwconv_kernel(st, width, rc, cc, u_ref, up_ref, un_ref, w_ref, b_ref, g_ref, beta_ref, o_ref,
                   win_ref, sh_ref, acc_ref):
    i = pl.program_id(1)
    valid, prev_ok, next_ok, _, _ = _segment_masks(st, i)
    tm, d = st.tm, st.d
    pad = width // 2
    win_ref[0:HALO, :] = jnp.where(prev_ok, up_ref[0], 0.0)
    win_ref[HALO:HALO + tm, :] = jnp.where(valid, u_ref[0], 0.0)
    win_ref[HALO + tm:, :] = jnp.where(next_ok, un_ref[0], 0.0)
    n_sh = sh_ref.shape[1]

    for c0 in range(0, d, cc):
        for r in range(1, 8):
            sh_ref[r - 1] = win_ref[r:r + n_sh, c0:c0 + cc]

        def row_chunk(rb, carry, c0=c0):
            r0 = pl.multiple_of(rb * rc, rc)
            acc = jnp.zeros((rc, cc), F32)
            for k in range(width):
                off = HALO - pad + k
                a8, res = 8 * (off // 8), off % 8
                if res == 0:
                    tap = win_ref[pl.ds(r0 + a8, rc), c0:c0 + cc]
                else:
                    tap = sh_ref[res - 1, pl.ds(r0 + a8, rc), :]
                acc = acc + tap * w_ref[k:k + 1, c0:c0 + cc]
            acc_ref[pl.ds(r0, rc), c0:c0 + cc] = acc + b_ref[:, c0:c0 + cc]
            return carry

        lax.fori_loop(0, tm // rc, row_chunk, 0)
    y = acc_ref[...]
    mu = jnp.mean(y, axis=-1, keepdims=True)
    yc = y - mu
    var = jnp.mean(yc * yc, axis=-1, keepdims=True)
    z = yc * lax.rsqrt(var + EPS) * g_ref[...] + beta_ref[...]
    o_ref[0] = _silu(z).astype(o_ref.dtype)


def _dwconv_call(st, u, dw_w, dw_b, ln_g, ln_b):
    width = dw_w.shape[0]
    assert width // 2 < HALO
    prev, nxt = _halo_specs(st)
    rc = 32
    cc = _divisor(st.d, 256, 128)
    n_sh = st.tm + 2 * HALO - 8
    kern = functools.partial(_dwconv_kernel, st, width, rc, cc)
    return pl.pallas_call(
        kern,
        grid=(st.b, st.nt),
        in_specs=[st.row_spec(st.d), prev, nxt,
                  pl.BlockSpec((width, st.d), lambda b, i: (0, 0)),
                  _vec_spec(st.d), _vec_spec(st.d), _vec_spec(st.d)],
        out_specs=st.row_spec(st.d),
        out_shape=jax.ShapeDtypeStruct((st.b, st.l, st.d), BF16),
        scratch_shapes=[pltpu.VMEM((st.tm + 2 * HALO, st.d), F32), pltpu.VMEM((7, n_sh, cc), F32),
                        pltpu.VMEM((st.tm, st.d), F32)],
        compiler_params=_cp("parallel", "parallel"),
        name="dwconv_ln_silu",
    )(u, u, u, dw_w, dw_b.reshape(1, -1), ln_g.reshape(1, -1), ln_b.reshape(1, -1))


def _pool_kernel(st, x_ref, xp_ref, xn_ref, g_ref, sc_ref, sh_ref, w_ref, ps_ref, gate_ref, o_ref, win_ref):
    i = pl.program_id(1)
    valid, prev_ok, next_ok, pos, seg_len = _segment_masks(st, i)
    tm, d = st.tm, st.d
    g, sc, sh = g_ref[...], sc_ref[0], sh_ref[0]
    x = x_ref[0]
    n = jnp.where(valid, _rms_mod(x, g, sc, sh), 0.0)
    win_ref[0:HALO, :] = jnp.where(prev_ok, _rms_mod(xp_ref[0], g, sc, sh), 0.0)
    win_ref[HALO:HALO + tm, :] = n
    win_ref[HALO + tm:, :] = jnp.where(next_ok, _rms_mod(xn_ref[0], g, sc, sh), 0.0)
    grp = d // len(POOL_WINDOWS)
    for gi, w in enumerate(POOL_WINDOWS):
        c0 = gi * grp
        tot = jnp.zeros((tm, grp), F32)
        for j in range(-(w // 2), w - w // 2):
            tot = tot + win_ref[HALO + j:HALO + j + tm, c0:c0 + grp]
        cnt = jnp.minimum(pos - w // 2 + w, seg_len) - jnp.maximum(pos - w // 2, 0)
        cnt = jnp.maximum(cnt, 1).astype(F32)
        diff = (tot / cnt - n[:, c0:c0 + grp]).astype(BF16)
        y = jnp.dot(diff, w_ref[gi], preferred_element_type=F32) * ps_ref[:, c0:c0 + grp]
        o_ref[0, :, c0:c0 + grp] = x[:, c0:c0 + grp] + gate_ref[0][:, c0:c0 + grp] * y


def _pool_call(st, x, g, sc, sh, pool_w, pool_scale, gate):
    prev, nxt = _halo_specs(st)
    ng, grp, _ = pool_w.shape
    return pl.pallas_call(
        functools.partial(_pool_kernel, st),
        grid=(st.b, st.nt),
        in_specs=[st.row_spec(st.d), prev, nxt, _vec_spec(st.d), st.mod_spec(), st.mod_spec(),
                  pl.BlockSpec((ng, grp, grp), lambda b, i: (0, 0, 0)), _vec_spec(st.d), st.mod_spec()],
        out_specs=st.row_spec(st.d),
        out_shape=jax.ShapeDtypeStruct((st.b, st.l, st.d), F32),
        scratch_shapes=[pltpu.VMEM((st.tm + 2 * HALO, st.d), F32)],
        compiler_params=_cp("parallel", "parallel"),
        name="pool_mixer",
    )(x, x, x, g.reshape(1, -1), sc, sh, pool_w, pool_scale.reshape(1, -1), gate)


def _rope_tables(st, rot_dim):
    quarter = rot_dim // 4
    rows = st.s // GRID_W
    inv = ROPE_THETA ** (-jnp.arange(quarter, dtype=F32) / quarter)
    row_ang = jnp.arange(rows, dtype=F32)[:, None, None] * inv
    col_ang = jnp.arange(GRID_W, dtype=F32)[None, :, None] * inv
    ang = jnp.concatenate([jnp.broadcast_to(row_ang, (rows, GRID_W, quarter)),
                           jnp.broadcast_to(col_ang, (rows, GRID_W, quarter))], axis=-1).reshape(st.s, 2 * quarter)
    cos, sin = jnp.cos(ang), jnp.sin(ang)
    c_tab = jnp.concatenate([cos, cos], axis=-1)
    s_tab = jnp.concatenate([-sin, sin], axis=-1)
    c_tab = jnp.concatenate([c_tab, jnp.ones((st.ctx, rot_dim), F32)], axis=0)
    s_tab = jnp.concatenate([s_tab, jnp.zeros((st.ctx, rot_dim), F32)], axis=0)
    return c_tab, s_tab


def _tab_spec(st, width):
    return pl.BlockSpec((st.tm, width), lambda b, i, *_: (i, 0))


def _folded_rope(r, cs):
    t = r * cs
    t = t + pltpu.roll(t, 64, axis=1)
    lane = lax.broadcasted_iota(jnp.int32, t.shape, 1)
    return jnp.where(lane < MLA_ROPE, t, 0.0)


def _mla_down_kernel(q_rank, kv_rank, n_ref, w_ref, qn_ref, kvn_ref, cs_ref, cq_ref, ckv_ref, kr_ref):
    acc = jnp.dot(n_ref[0], w_ref[...], preferred_element_type=F32)
    cq = acc[:, :q_rank]
    cq_ref[0] = (cq * lax.rsqrt(jnp.mean(cq * cq, axis=-1, keepdims=True) + EPS) * qn_ref[...]).astype(BF16)
    ckv = acc[:, q_rank:q_rank + kv_rank]
    ckv_ref[0] = (ckv * lax.rsqrt(jnp.mean(ckv * ckv, axis=-1, keepdims=True) + EPS) * kvn_ref[...]).astype(BF16)
    kr_ref[0] = _folded_rope(acc[:, q_rank + kv_rank:], cs_ref[...]).astype(BF16)


def _mla_down_call(st, n, w_cat, q_norm, kv_norm, cs_tab):
    k, ncols = w_cat.shape
    q_rank, kv_rank = q_norm.shape[0], kv_norm.shape[0]
    assert ncols == q_rank + kv_rank + 128 and q_rank % 128 == 0 and kv_rank % 128 == 0
    return pl.pallas_call(
        functools.partial(_mla_down_kernel, q_rank, kv_rank),
        grid=(st.b, st.nt),
        in_specs=[st.row_spec(k), pl.BlockSpec((k, ncols), lambda b, i: (0, 0)),
                  _vec_spec(q_rank), _vec_spec(kv_rank), _tab_spec(st, 128)],
        out_specs=[st.row_spec(q_rank), st.row_spec(kv_rank), st.row_spec(128)],
        out_shape=[jax.ShapeDtypeStruct((st.b, st.l, q_rank), BF16),
                   jax.ShapeDtypeStruct((st.b, st.l, kv_rank), BF16),
                   jax.ShapeDtypeStruct((st.b, st.l, 128), BF16)],
        compiler_params=_cp("parallel", "parallel"),
        name="mla_down",
    )(n, w_cat, q_norm.reshape(1, -1), kv_norm.reshape(1, -1), cs_tab)


def _mla_uq_kernel(heads, qscale, cq_ref, w_ref, cs_ref, qt_ref):
    cq, cs = cq_ref[0], cs_ref[...]
    for h in range(heads):
        acc = jnp.dot(cq, w_ref[:, h * MLA_QK_PAD:(h + 1) * MLA_QK_PAD], preferred_element_type=F32)
        q = jnp.concatenate([acc[:, :MLA_NOPE], _folded_rope(acc[:, MLA_NOPE:], cs)], axis=1) * qscale
        qt_ref[0, h] = q.T.astype(BF16)


def _mla_uq_call(st, cq, w_ext, cs_tab, heads, qscale):
    k = cq.shape[-1]
    return pl.pallas_call(
        functools.partial(_mla_uq_kernel, heads, qscale),
        grid=(st.b, st.nt),
        in_specs=[st.row_spec(k), pl.BlockSpec((k, heads * MLA_QK_PAD), lambda b, i: (0, 0)), _tab_spec(st, 128)],
        out_specs=pl.BlockSpec((1, heads, MLA_QK_PAD, st.tm), lambda b, i: (b, 0, 0, i)),
        out_shape=jax.ShapeDtypeStruct((st.b, heads, MLA_QK_PAD, st.l), BF16),
        compiler_params=_cp("parallel", "parallel"),
        name="mla_up_q",
    )(cq, w_ext, cs_tab)


def _mla_ukv_kernel(heads, ckv_ref, w_ref, kr_ref, k_ref, vt_ref):
    ckv, kr = ckv_ref[0], kr_ref[0]
    hw = MLA_NOPE + MLA_V
    for h in range(heads):
        acc = jnp.dot(ckv, w_ref[:, h * hw:(h + 1) * hw], preferred_element_type=F32)
        k_ref[0, h, :, :MLA_NOPE] = acc[:, :MLA_NOPE].astype(BF16)
        k_ref[0, h, :, MLA_NOPE:] = kr
        vt_ref[0, h] = acc[:, MLA_NOPE:].T.astype(BF16)


def _mla_ukv_call(st, ckv, w_ukv, kr, heads):
    k = ckv.shape[-1]
    hw = MLA_NOPE + MLA_V
    return pl.pallas_call(
        functools.partial(_mla_ukv_kernel, heads),
        grid=(st.b, st.nt),
        in_specs=[st.row_spec(k), pl.BlockSpec((k, heads * hw), lambda b, i: (0, 0)), st.row_spec(128)],
        out_specs=[pl.BlockSpec((1, heads, st.tm, MLA_QK_PAD), lambda b, i: (b, 0, i, 0)),
                   pl.BlockSpec((1, heads, MLA_V, st.tm), lambda b, i: (b, 0, 0, i))],
        out_shape=[jax.ShapeDtypeStruct((st.b, heads, st.l, MLA_QK_PAD), BF16),
                   jax.ShapeDtypeStruct((st.b, heads, MLA_V, st.l), BF16)],
        compiler_params=_cp("parallel", "parallel"),
        name="mla_up_kv",
    )(ckv, w_ukv, kr)


def _softmax_chunk_t(st, tk, s_ref, j, vt, m_ref, l_ref, acc_ref, masked):
    s = s_ref[...]
    if masked:
        row = j * tk + lax.broadcasted_iota(jnp.int32, s.shape, 0)
        s = jnp.where(row >= st.s, s, NEG_BIG)
    m_prev = m_ref[...]
    m_new = jnp.maximum(m_prev, jnp.max(s, axis=0, keepdims=True))
    alpha = jnp.exp2(m_prev - m_new)
    p = jnp.exp2(s - m_new)
    l_ref[...] = alpha * l_ref[...] + jnp.sum(p, axis=0, keepdims=True)
    acc_ref[...] = alpha * acc_ref[...] + jnp.dot(vt, p.astype(BF16), preferred_element_type=F32)
    m_ref[...] = m_new


def _pipelined_key_loop(st, tq, tk, qi, qk, proc, bufs0, bufs1):
    n_lat_q = st.s // tq
    nch = st.l // tk

    @pl.when(qi < n_lat_q)
    def _():
        qk(0, bufs0)

        def pair(jj, carry):
            qk(2 * jj + 1, bufs1)
            proc(2 * jj, bufs0, False)
            qk(2 * jj + 2, bufs0)
            proc(2 * jj + 1, bufs1, False)
            return carry

        lax.fori_loop(0, (nch - 1) // 2, pair, 0)
        if nch % 2 == 1:
            proc(nch - 1, bufs0, False)
        else:
            qk(nch - 1, bufs1)
            proc(nch - 2, bufs0, False)
            proc(nch - 1, bufs1, False)

    @pl.when(qi == n_lat_q)
    def _():
        for j in range(st.s // tk, nch):
            qk(j, bufs0)
            proc(j, bufs0, True)


def _chunk(j, tk):
    return pl.ds(pl.multiple_of(j * tk, tk), tk)


def _mla_attn_kernel(st, tq, tk, qt_ref, k_ref, vt_ref, o_ref, s0_ref, s1_ref, m_ref, l_ref, acc_ref):
    m_ref[...] = jnp.full_like(m_ref, NEG_BIG)
    l_ref[...] = jnp.zeros_like(l_ref)
    acc_ref[...] = jnp.zeros_like(acc_ref)

    def qk(j, s_ref):
        s_ref[...] = jnp.dot(k_ref[0, 0, _chunk(j, tk), :], qt_ref[0, 0], preferred_element_type=F32)

    def proc(j, s_ref, masked):
        _softmax_chunk_t(st, tk, s_ref, j, vt_ref[0, 0, :, _chunk(j, tk)], m_ref, l_ref, acc_ref, masked)

    _pipelined_key_loop(st, tq, tk, pl.program_id(2), qk, proc, s0_ref, s1_ref)
    o_ref[0] = (acc_ref[...] / l_ref[...]).T.astype(o_ref.dtype)


def _attn_tiles(st, tq_target):
    tq = _divisor(st.s, tq_target, 128)
    assert st.ctx <= tq
    tk = _divisor(st.l, 1536, 256 if st.l % 256 == 0 else 128)
    return tq, tk


def _mla_attn_call(st, qt, k, vt, heads):
    tq, tk = _attn_tiles(st, 1024)
    return pl.pallas_call(
        functools.partial(_mla_attn_kernel, st, tq, tk),
        grid=(st.b, heads, st.s // tq + 1),
        in_specs=[pl.BlockSpec((1, 1, MLA_QK_PAD, tq), lambda b, h, i: (b, h, 0, i)),
                  pl.BlockSpec((1, 1, st.l, MLA_QK_PAD), lambda b, h, i: (b, h, 0, 0)),
                  pl.BlockSpec((1, 1, MLA_V, st.l), lambda b, h, i: (b, h, 0, 0))],
        out_specs=pl.BlockSpec((1, tq, MLA_V), lambda b, h, i: (b, i, h)),
        out_shape=jax.ShapeDtypeStruct((st.b, st.l, heads * MLA_V), BF16),
        scratch_shapes=[pltpu.VMEM((tk, tq), F32), pltpu.VMEM((tk, tq), F32),
                        pltpu.VMEM((1, tq), F32), pltpu.VMEM((1, tq), F32), pltpu.VMEM((MLA_V, tq), F32)],
        compiler_params=_cp("parallel", "parallel", "arbitrary"),
        name="mla_attention",
    )(qt, k, vt)


def _diff_proj_kernel(heads, mode, qscale, n_ref, w_ref, c_ref, s_ref, o_ref):
    hd = DIFF_HEAD_DIM
    acc = jnp.dot(n_ref[0], w_ref[...], preferred_element_type=F32)
    if mode != "v":
        c, s = c_ref[...], s_ref[...]
    for h in range(heads):
        if mode == "v":
            o_ref[0, h] = acc[:, 2 * h * hd:2 * (h + 1) * hd].T.astype(BF16)
            continue
        halves = []
        for n in range(2):
            xh = acc[:, (2 * h + n) * hd:(2 * h + n + 1) * hd]
            halves.append(xh * c + pltpu.roll(xh, hd // 2, axis=1) * s)
        if mode == "k":
            o_ref[0, :, 2 * h * hd:(2 * h + 1) * hd] = halves[0].astype(BF16)
            o_ref[0, :, (2 * h + 1) * hd:2 * (h + 1) * hd] = halves[1].astype(BF16)
        else:
            o_ref[0, h] = (jnp.concatenate(halves, axis=1) * qscale).T.astype(BF16)


def _diff_proj_call(st, n, w_qkv, c_tab, s_tab, heads, mode, qscale=1.0):
    k = w_qkv.shape[0]
    hw = 2 * DIFF_HEAD_DIM
    third = {"q": 0, "k": 1, "v": 2}[mode]
    if mode == "k":
        out_spec = st.row_spec(heads * hw)
        out_shape = jax.ShapeDtypeStruct((st.b, st.l, heads * hw), BF16)
    else:
        out_spec = pl.BlockSpec((1, heads, hw, st.tm), lambda b, i: (b, 0, 0, i))
        out_shape = jax.ShapeDtypeStruct((st.b, heads, hw, st.l), BF16)
    return pl.pallas_call(
        functools.partial(_diff_proj_kernel, heads, mode, qscale),
        grid=(st.b, st.nt),
        in_specs=[st.row_spec(k), pl.BlockSpec((k, heads * hw), lambda b, i: (0, third)),
                  _tab_spec(st, DIFF_HEAD_DIM), _tab_spec(st, DIFF_HEAD_DIM)],
        out_specs=out_spec,
        out_shape=out_shape,
        compiler_params=_cp("parallel", "parallel"),
        name="diff_proj_" + mode,
    )(n, w_qkv, c_tab, s_tab)


def _diff_attn_kernel(st, tq, tk, lam_init, qt_ref, k_ref, vt_ref, lq1_ref, lk1_ref, lq2_ref, lk2_ref, sg_ref, o_ref,
                      sa0, sb0, sa1, sb1, m1, l1, a1, m2, l2, a2):
    hd = DIFF_HEAD_DIM
    for m_ref, l_ref, a_ref in ((m1, l1, a1), (m2, l2, a2)):
        m_ref[...] = jnp.full_like(m_ref, NEG_BIG)
        l_ref[...] = jnp.zeros_like(l_ref)
        a_ref[...] = jnp.zeros_like(a_ref)

    def qk(j, bufs):
        kc = k_ref[0, _chunk(j, tk), :]
        bufs[0][...] = jnp.dot(kc[:, :hd], qt_ref[0, 0, :hd, :], preferred_element_type=F32)
        bufs[1][...] = jnp.dot(kc[:, hd:], qt_ref[0, 0, hd:, :], preferred_element_type=F32)

    def proc(j, bufs, masked):
        vt = vt_ref[0, 0, :, _chunk(j, tk)]
        _softmax_chunk_t(st, tk, bufs[0], j, vt, m1, l1, a1, masked)
        _softmax_chunk_t(st, tk, bufs[1], j, vt, m2, l2, a2, masked)

    _pipelined_key_loop(st, tq, tk, pl.program_id(2), qk, proc, (sa0, sb0), (sa1, sb1))
    lam = (jnp.exp(jnp.sum(lq1_ref[...] * lk1_ref[...], axis=-1, keepdims=True))
           - jnp.exp(jnp.sum(lq2_ref[...] * lk2_ref[...], axis=-1, keepdims=True)) + lam_init)
    o = a1[...] / l1[...] - lam * (a2[...] / l2[...])
    o = o * lax.rsqrt(jnp.mean(o * o, axis=0, keepdims=True) + EPS) * sg_ref[...]
    o_ref[0] = (o * (1.0 - lam_init)).T.astype(o_ref.dtype)


def _diff_attn_call(st, qt, k, vt, lq1, lk1, lq2, lk2, subln_g, heads, lam_init):
    tq, tk = _attn_tiles(st, 512)
    hw = 2 * DIFF_HEAD_DIM
    vec = lambda a: a.reshape(1, -1)
    return pl.pallas_call(
        functools.partial(_diff_attn_kernel, st, tq, tk, lam_init),
        grid=(st.b, heads, st.s // tq + 1),
        in_specs=[pl.BlockSpec((1, 1, hw, tq), lambda b, h, i: (b, h, 0, i)),
                  pl.BlockSpec((1, st.l, hw), lambda b, h, i: (b, 0, h)),
                  pl.BlockSpec((1, 1, hw, st.l), lambda b, h, i: (b, h, 0, 0)),
                  _vec_spec(DIFF_HEAD_DIM), _vec_spec(DIFF_HEAD_DIM), _vec_spec(DIFF_HEAD_DIM),
                  _vec_spec(DIFF_HEAD_DIM), pl.BlockSpec((hw, 1), lambda *_: (0, 0))],
        out_specs=pl.BlockSpec((1, tq, hw), lambda b, h, i: (b, i, h)),
        out_shape=jax.ShapeDtypeStruct((st.b, st.l, heads * hw), BF16),
        scratch_shapes=[pltpu.VMEM((tk, tq), F32)] * 4
        + [pltpu.VMEM((1, tq), F32), pltpu.VMEM((1, tq), F32), pltpu.VMEM((hw, tq), F32)] * 2,
        compiler_params=_cp("parallel", "parallel", "arbitrary"),
        name="diff_attention",
    )(qt, k, vt, vec(lq1), vec(lk1), vec(lq2), vec(lk2), subln_g.reshape(-1, 1))


def _expert_kernel(tm, n_steps, idx_hbm, x_ref, wg_ref, wu_ref, wd_ref, g_ref, gate_ref, acc_in, acc_hbm,
                   idx_smem, rows_ref, sem_idx, sem_gather, sem_scatter):
    del acc_in
    k = pl.program_id(0) * pl.num_programs(1) + pl.program_id(1)
    slot = lax.rem(k, 2)

    def idx_copy(step, s):
        return pltpu.make_async_copy(idx_hbm.at[step], idx_smem.at[s], sem_idx.at[s])

    def row_copy(r, sem, to_hbm):
        hbm, vmem = acc_hbm.at[pl.ds(idx_smem[slot, r], 1), :], rows_ref.at[pl.ds(r, 1), :]
        return pltpu.make_async_copy(vmem, hbm, sem) if to_hbm else pltpu.make_async_copy(hbm, vmem, sem)

    def all_rows(sem):
        return pltpu.make_async_copy(acc_hbm.at[pl.ds(0, tm), :], rows_ref, sem)

    @pl.when(k == 0)
    def _():
        idx_copy(0, 0).start()

    idx_copy(k, slot).wait()

    @pl.when(k + 1 < n_steps)
    def _():
        idx_copy(k + 1, 1 - slot).start()

    @pl.when(k > 0)
    def _():
        all_rows(sem_scatter).wait()

    for r in range(tm):
        row_copy(r, sem_gather, False).start()

    x = x_ref[0]
    a = jnp.dot(x, wg_ref[0], preferred_element_type=F32)
    u = jnp.dot(x, wu_ref[0], preferred_element_type=F32)
    h = (_silu(a) * u).astype(BF16)
    y = jnp.dot(h, wd_ref[0], preferred_element_type=F32) * g_ref[0]

    all_rows(sem_gather).wait()
    rows_ref[...] = rows_ref[...] + gate_ref[0] * y
    for r in range(tm):
        row_copy(r, sem_scatter, True).start()

    @pl.when(k == n_steps - 1)
    def _():
        all_rows(sem_scatter).wait()


def _expert_call(xs, wg, wu, wd, gates, idx, acc, gate, tm, gate_row0):
    n_e, rows, d = xs.shape
    ff = wg.shape[-1]
    nt = rows // tm
    tiles_per_sample = nt // (gate.shape[0] // 2)
    params = pltpu.CompilerParams(dimension_semantics=("arbitrary", "arbitrary"), vmem_limit_bytes=VMEM_LIMIT_BYTES,
                                  disable_bounds_checks=True)
    return pl.pallas_call(
        functools.partial(_expert_kernel, tm, n_e * nt),
        grid=(n_e, nt),
        in_specs=[pl.BlockSpec(memory_space=pl.ANY),
                  pl.BlockSpec((1, tm, d), lambda e, i: (e, i, 0)),
                  pl.BlockSpec((1, d, ff), lambda e, i: (e, 0, 0)),
                  pl.BlockSpec((1, d, ff), lambda e, i: (e, 0, 0)),
                  pl.BlockSpec((1, ff, d), lambda e, i: (e, 0, 0)),
                  pl.BlockSpec((1, tm, 1), lambda e, i: (e, i, 0)),
                  pl.BlockSpec((1, 1, d), lambda e, i: ((i // tiles_per_sample) * 2 + gate_row0, 0, 0)),
                  pl.BlockSpec(memory_space=pl.ANY)],
        out_specs=pl.BlockSpec(memory_space=pl.ANY),
        out_shape=jax.ShapeDtypeStruct(acc.shape, F32),
        input_output_aliases={7: 0},
        scratch_shapes=[pltpu.SMEM((2, tm), jnp.int32), pltpu.VMEM((tm, d), F32),
                        pltpu.SemaphoreType.DMA((2,)), pltpu.SemaphoreType.DMA, pltpu.SemaphoreType.DMA],
        compiler_params=params,
        name="moe_experts",
    )(idx.reshape(n_e * nt, tm), xs, wg, wu, wd, gates, gate, acc)


def _route(st, m, aff_seg, offset, cap):
    n_e = aff_seg.shape[1]
    g_sel, idx = lax.top_k(aff_seg, cap)
    idx = idx + offset
    bidx = jnp.arange(st.b)[:, None, None]
    xs = jnp.swapaxes(m[bidx, idx], 0, 1).reshape(n_e, st.b * cap, st.d)
    gates = jnp.swapaxes(g_sel, 0, 1).reshape(n_e, st.b * cap, 1)
    rows = jnp.swapaxes(idx + bidx * st.l, 0, 1).reshape(n_e, st.b * cap).astype(jnp.int32)
    return xs, gates, rows


def _moe(st, x, g, sc, sh, gate, router_w, wg, wu, wd, with_ctx):
    n_e = router_w.shape[-1]
    m, aff = _norm_router_call(st, x, g, sc, sh, router_w.T.astype(BF16))
    wg, wu, wd = wg.astype(BF16), wu.astype(BF16), wd.astype(BF16)
    acc = x.reshape(st.b * st.l, st.d)
    cap_l = max(1, EC_FACTOR * st.s // n_e)
    xs, gates, rows = _route(st, m, aff[:, :, :st.s], 0, cap_l)
    acc = _expert_call(xs, wg, wu, wd, gates, rows, acc, gate, _divisor(cap_l, 256, 16), 0)
    if with_ctx:
        cap_c = max(1, EC_FACTOR * st.ctx // n_e)
        xs, gates, rows = _route(st, m, aff[:, :, st.s:], st.s, cap_c)
        acc = _expert_call(xs, wg, wu, wd, gates, rows, acc, gate, _divisor(cap_c, 256, 16), 1)
    return acc.reshape(st.b, st.l, st.d)


def _swap_halves_cols(w, width):
    k, n = w.shape
    w = w.reshape(k, n // width, 2, width // 2)
    return w[:, :, ::-1, :].reshape(k, n)


def kernel(x, c, ctx, c_ctx, ada_w, ada_b, norm_g, final_g, conv_pw1_w, conv_pw1_b, conv_dw_w, conv_dw_b, conv_ln_g, conv_ln_b, conv_pw2_w, conv_pw2_b, mla_w_dq, mla_q_norm, mla_w_uq, mla_w_dkv, mla_kv_norm, mla_w_ukv, mla_w_o, pool_w, pool_scale, diff_w_qkv, diff_lq1, diff_lk1, diff_lq2, diff_lk2, diff_subln_g, diff_w_o, moe_router, moe_w_gate, moe_w_up, moe_w_down):
    b, s, d = x.shape
    n_ctx = ctx.shape[1]
    depth = ada_w.shape[0]
    st = _Stream(b, s, n_ctx, d, _divisor(s, 512, max(n_ctx, 128)))
    xs = jnp.concatenate([x, ctx], axis=1)

    c8 = jnp.zeros((8, d), F32).at[:b].set(c).at[b].set(c_ctx)
    mod = _mod_call(c8, ada_w, ada_b)
    order = jnp.stack([jnp.arange(b), jnp.full((b,), b)], axis=1).reshape(-1)
    mod = mod[:, order, :].reshape(depth, 2 * b, 6, d)
    mods = [[mod[i, :, k, :].reshape(2 * b, 1, d) for k in range(6)] for i in range(depth)]

    heads = mla_w_uq.shape[1] // (MLA_NOPE + MLA_ROPE)
    q_rank, kv_rank = mla_q_norm.shape[0], mla_kv_norm.shape[0]
    diff_heads = diff_w_o.shape[0] // (2 * DIFF_HEAD_DIM)
    zeros_d = jnp.zeros((d,), F32)

    for i in range(depth):
        kind = i % 4
        need_ctx_out = i < depth - 1
        sh1, sc1, gt1, sh2, sc2, gt2 = mods[i]
        if kind == 0:
            n = _norm_call(st, xs, norm_g[i, 0], sc1, sh1)
            u = _glu_call(st, n, conv_pw1_w.astype(BF16), conv_pw1_b)
            v = _dwconv_call(st, u, conv_dw_w, conv_dw_b, conv_ln_g, conv_ln_b)
            xs = _proj_res_call(st, v, conv_pw2_w.astype(BF16), conv_pw2_b, xs, gt1)
        elif kind == 1:
            n = _norm_call(st, xs, norm_g[i, 0], sc1, sh1)
            c_tab, s_tab = _rope_tables(st, MLA_ROPE)
            cs_tab = jnp.concatenate([c_tab, s_tab], axis=-1)
            w_kr = mla_w_dkv[:, kv_rank:]
            w_cat = jnp.concatenate([mla_w_dq, mla_w_dkv[:, :kv_rank], w_kr, _swap_halves_cols(w_kr, MLA_ROPE)],
                                    axis=1).astype(BF16)
            cq, ckv, kr = _mla_down_call(st, n, w_cat, mla_q_norm, mla_kv_norm, cs_tab)
            w_uq = mla_w_uq.reshape(q_rank, heads, MLA_NOPE + MLA_ROPE)
            w_uq_rope = w_uq[:, :, MLA_NOPE:].reshape(q_rank, heads * MLA_ROPE)
            w_uq_ext = jnp.concatenate(
                [w_uq, _swap_halves_cols(w_uq_rope, MLA_ROPE).reshape(q_rank, heads, MLA_ROPE)], axis=-1)
            qscale = (MLA_NOPE + MLA_ROPE) ** -0.5 * LOG2E
            qt = _mla_uq_call(st, cq, w_uq_ext.reshape(q_rank, heads * MLA_QK_PAD).astype(BF16), cs_tab, heads, qscale)
            k, vt = _mla_ukv_call(st, ckv, mla_w_ukv.astype(BF16), kr, heads)
            o = _mla_attn_call(st, qt, k, vt, heads)
            xs = _proj_res_call(st, o, mla_w_o.astype(BF16), zeros_d, xs, gt1)
        elif kind == 2:
            xs = _pool_call(st, xs, norm_g[i, 0], sc1, sh1, pool_w.astype(BF16), pool_scale, gt1)
        else:
            n = _norm_call(st, xs, norm_g[i, 0], sc1, sh1)
            c_tab, s_tab = _rope_tables(st, DIFF_HEAD_DIM)
            w_qkv = diff_w_qkv.astype(BF16)
            qt = _diff_proj_call(st, n, w_qkv, c_tab, s_tab, diff_heads, "q", DIFF_HEAD_DIM ** -0.5 * LOG2E)
            k = _diff_proj_call(st, n, w_qkv, c_tab, s_tab, diff_heads, "k")
            vt = _diff_proj_call(st, n, w_qkv, c_tab, s_tab, diff_heads, "v")
            lam_init = 0.8 - 0.6 * math.exp(-0.3 * i)
            o = _diff_attn_call(st, qt, k, vt, diff_lq1, diff_lk1, diff_lq2, diff_lk2, diff_subln_g, diff_heads,
                                lam_init)
            xs = _proj_res_call(st, o, diff_w_o.astype(BF16), zeros_d, xs, gt1)
        xs = _moe(st, xs, norm_g[i, 1], sc2, sh2, gt2, moe_router[i], moe_w_gate[i], moe_w_up[i], moe_w_down[i],
                  need_ctx_out)
    return _final_norm_call(st, xs, final_g)
```

```python
import functools
import math

import jax
import jax.numpy as jnp
from jax import lax
from jax.experimental import pallas as pl
from jax.experimental.pallas import tpu as pltpu

F32 = jnp.float32
BF16 = jnp.bfloat16

GRID_W = 64
EPS = 1e-6
ROPE_THETA = 10000.0
MLA_NOPE = 128
MLA_ROPE = 64
MLA_V = 128
MLA_QK_PAD = 256
DIFF_HEAD_DIM = 128
POOL_WINDOWS = (2, 4, 8, 16)
EC_FACTOR = 2
HALO = 16
NEG_BIG = -1e30
LOG2E = 1.4426950408889634

VMEM_LIMIT_BYTES = 56 * 1024 * 1024


def _cp(*sem):
    return pltpu.CompilerParams(dimension_semantics=sem, vmem_limit_bytes=VMEM_LIMIT_BYTES)


def _divisor(n, target, mult):
    best = None
    d = mult
    while d <= min(n, target):
        if n % d == 0:
            best = d
        d += mult
    return best if best is not None else n


def _silu(v):
    return v * jax.nn.sigmoid(v)


def _rms_mod(x, g, sc, sh):
    ms = jnp.mean(x * x, axis=-1, keepdims=True)
    return (x * lax.rsqrt(ms + EPS) * g) * (1.0 + sc) + sh


def _mod_kernel(c_ref, w_ref, b_ref, o_ref):
    s = _silu(c_ref[...]).astype(BF16)
    o_ref[0] = jnp.dot(s, w_ref[0].astype(BF16), preferred_element_type=F32) + b_ref[0]


def _mod_call(c8, ada_w, ada_b):
    depth, d, n = ada_w.shape
    tn = _divisor(n, 1024, 128)
    return pl.pallas_call(
        _mod_kernel,
        grid=(depth, n // tn),
        in_specs=[pl.BlockSpec((8, d), lambda l, j: (0, 0)),
                  pl.BlockSpec((1, d, tn), lambda l, j: (l, 0, j)),
                  pl.BlockSpec((1, 1, tn), lambda l, j: (l, 0, j))],
        out_specs=pl.BlockSpec((1, 8, tn), lambda l, j: (l, 0, j)),
        out_shape=jax.ShapeDtypeStruct((depth, 8, n), F32),
        compiler_params=_cp("arbitrary", "arbitrary"),
        name="adaln_mod",
    )(c8, ada_w, ada_b.reshape(depth, 1, n))


class _Stream:
    def __init__(self, b, s, ctx, d, tm):
        assert s % tm == 0 and ctx <= tm and ctx % HALO == 0 and tm % HALO == 0
        self.b, self.s, self.ctx, self.d, self.tm = b, s, ctx, d, tm
        self.l = s + ctx
        self.n_lat = s // tm
        self.nt = self.n_lat + 1

    def row_spec(self, width, col=None):
        if col is None:
            return pl.BlockSpec((1, self.tm, width), lambda b, i, *_: (b, i, 0))
        return pl.BlockSpec((1, self.tm, width), lambda b, i, *_: (b, i, col))

    def mod_spec(self):
        n_lat = self.n_lat
        return pl.BlockSpec((1, 1, self.d), lambda b, i, *_: (b * 2 + i // n_lat, 0, 0))


def _vec_spec(width):
    return pl.BlockSpec((1, width), lambda *_: (0, 0))


def _norm_kernel(x_ref, g_ref, sc_ref, sh_ref, o_ref):
    o_ref[0] = _rms_mod(x_ref[0], g_ref[...], sc_ref[0], sh_ref[0]).astype(o_ref.dtype)


def _norm_call(st, x, g, sc, sh):
    return pl.pallas_call(
        _norm_kernel,
        grid=(st.b, st.nt),
        in_specs=[st.row_spec(st.d), _vec_spec(st.d), st.mod_spec(), st.mod_spec()],
        out_specs=st.row_spec(st.d),
        out_shape=jax.ShapeDtypeStruct((st.b, st.l, st.d), BF16),
        compiler_params=_cp("parallel", "parallel"),
        name="norm_mod",
    )(x, g.reshape(1, -1), sc, sh)


def _norm_router_kernel(x_ref, g_ref, sc_ref, sh_ref, rw_ref, m_ref, aff_ref):
    m = _rms_mod(x_ref[0], g_ref[...], sc_ref[0], sh_ref[0]).astype(BF16)
    m_ref[0] = m
    logits = lax.dot_general(rw_ref[...], m, (((1,), (1,)), ((), ())), preferred_element_type=F32)
    z = logits - jnp.max(logits, axis=0, keepdims=True)
    e = jnp.exp(z)
    aff_ref[0] = e / jnp.sum(e, axis=0, keepdims=True)


def _norm_router_call(st, x, g, sc, sh, router_t):
    n_e = router_t.shape[0]
    return pl.pallas_call(
        _norm_router_kernel,
        grid=(st.b, st.nt),
        in_specs=[st.row_spec(st.d), _vec_spec(st.d), st.mod_spec(), st.mod_spec(),
                  pl.BlockSpec((n_e, st.d), lambda b, i: (0, 0))],
        out_specs=[st.row_spec(st.d), pl.BlockSpec((1, n_e, st.tm), lambda b, i: (b, 0, i))],
        out_shape=[jax.ShapeDtypeStruct((st.b, st.l, st.d), BF16),
                   jax.ShapeDtypeStruct((st.b, n_e, st.l), F32)],
        compiler_params=_cp("parallel", "parallel"),
        name="moe_norm_router",
    )(x, g.reshape(1, -1), sc, sh, router_t)


def _final_norm_kernel(x_ref, g_ref, o_ref):
    x = x_ref[0]
    ms = jnp.mean(x * x, axis=-1, keepdims=True)
    o_ref[0] = x * lax.rsqrt(ms + EPS) * g_ref[...]


def _final_norm_call(st, x, g):
    return pl.pallas_call(
        _final_norm_kernel,
        grid=(st.b, st.n_lat),
        in_specs=[st.row_spec(st.d), _vec_spec(st.d)],
        out_specs=st.row_spec(st.d),
        out_shape=jax.ShapeDtypeStruct((st.b, st.s, st.d), F32),
        compiler_params=_cp("parallel", "parallel"),
        name="final_norm",
    )(x, g.reshape(1, -1))


def _proj_res_kernel(a_ref, w_ref, bias_ref, x_ref, gate_ref, o_ref):
    y = jnp.dot(a_ref[0], w_ref[...], preferred_element_type=F32) + bias_ref[...]
    o_ref[0] = x_ref[0] + gate_ref[0] * y


def _proj_res_call(st, a, w, bias, x, gate):
    k, n = w.shape
    return pl.pallas_call(
        _proj_res_kernel,
        grid=(st.b, st.nt),
        in_specs=[st.row_spec(k), pl.BlockSpec((k, n), lambda b, i: (0, 0)), _vec_spec(n),
                  st.row_spec(n), st.mod_spec()],
        out_specs=st.row_spec(n),
        out_shape=jax.ShapeDtypeStruct((st.b, st.l, n), F32),
        compiler_params=_cp("parallel", "parallel"),
        name="proj_residual",
    )(a, w, bias.reshape(1, -1), x, gate)


def _glu_kernel(a_ref, wa_ref, wb_ref, ba_ref, bb_ref, o_ref):
    a = a_ref[0]
    u = jnp.dot(a, wa_ref[...], preferred_element_type=F32) + ba_ref[...]
    v = jnp.dot(a, wb_ref[...], preferred_element_type=F32) + bb_ref[...]
    o_ref[0] = u * jax.nn.sigmoid(v)


def _glu_call(st, a, w, bias):
    k, n2 = w.shape
    n = n2 // 2
    tn = _divisor(n, 512, 128)
    nj = n // tn
    b2 = bias.reshape(1, -1)
    return pl.pallas_call(
        _glu_kernel,
        grid=(st.b, st.nt, nj),
        in_specs=[st.row_spec(k),
                  pl.BlockSpec((k, tn), lambda b, i, j: (0, j)),
                  pl.BlockSpec((k, tn), lambda b, i, j: (0, j + nj)),
                  pl.BlockSpec((1, tn), lambda b, i, j: (0, j)),
                  pl.BlockSpec((1, tn), lambda b, i, j: (0, j + nj))],
        out_specs=pl.BlockSpec((1, st.tm, tn), lambda b, i, j: (b, i, j)),
        out_shape=jax.ShapeDtypeStruct((st.b, st.l, n), F32),
        compiler_params=_cp("parallel", "parallel", "arbitrary"),
        name="conv_pw1_glu",
    )(a, w, w, b2, b2)


def _halo_specs(st):
    r = st.tm // HALO
    last = st.l // HALO - 1
    prev = pl.BlockSpec((1, HALO, st.d), lambda b, i: (b, jnp.maximum(i * r - 1, 0), 0))
    nxt = pl.BlockSpec((1, HALO, st.d), lambda b, i: (b, jnp.minimum((i + 1) * r, last), 0))
    return prev, nxt


def _segment_masks(st, i):
    is_ctx = i == st.n_lat
    seg_len = jnp.where(is_ctx, st.ctx, st.s)
    base = jnp.where(is_ctx, 0, i * st.tm)
    rows = lax.broadcasted_iota(jnp.int32, (st.tm, 1), 0)
    pos = base + rows
    valid = pos < seg_len
    prev_ok = jnp.logical_and(i != 0, jnp.logical_not(is_ctx))
    next_ok = i < st.n_lat - 1
    return valid, prev_ok, next_ok, pos, seg_len


def _dwconv_kernel(st, width, rc, cc, u_ref, up_ref, un_ref, w_ref, b_ref, g_ref, beta_ref, o_ref,
                   win_ref, sh_ref, acc_ref):
    i = pl.program_id(1)
    valid, prev_ok, next_ok, _, _ = _segment_masks(st, i)
    tm, d = st.tm, st.d
    pad = width // 2
    win_ref[0:HALO, :] = jnp.where(prev_ok, up_ref[0], 0.0)
    win_ref[HALO:HALO + tm, :] = jnp.where(valid, u_ref[0], 0.0)
    win_ref[HALO + tm:, :] = jnp.where(next_ok, un_ref[0], 0.0)
    n_sh = sh_ref.shape[1]

    for c0 in range(0, d, cc):
        for r in range(1, 8):
            sh_ref[r - 1] = win_ref[r:r + n_sh, c0:c0 + cc]

        def row_chunk(rb, carry, c0=c0):
            r0 = pl.multiple_of(rb * rc, rc)
            acc = jnp.zeros((rc, cc), F32)
            for k in range(width):
                off = HALO - pad + k
                a8, res = 8 * (off // 8), off % 8
                if res == 0:
                    tap = win_ref[pl.ds(r0 + a8, rc), c0:c0 + cc]
                else:
                    tap = sh_ref[res - 1, pl.ds(r0 + a8, rc), :]
                acc = acc + tap * w_ref[k:k + 1, c0:c0 + cc]
            acc_ref[pl.ds(r0, rc), c0:c0 + cc] = acc + b_ref[:, c0:c0 + cc]
            return carry

        lax.fori_loop(0, tm // rc, row_chunk, 0)
    y = acc_ref[...]
    mu = jnp.mean(y, axis=-1, keepdims=True)
    yc = y - mu
    var = jnp.mean(yc * yc, axis=-1, keepdims=True)
    z = yc * lax.rsqrt(var + EPS) * g_ref[...] + beta_ref[...]
    o_ref[0] = _silu(z).astype(o_ref.dtype)


def _dwconv_call(st, u, dw_w, dw_b, ln_g, ln_b):
    width = dw_w.shape[0]
    assert width // 2 < HALO
    prev, nxt = _halo_specs(st)
    rc = 32
    cc = _divisor(st.d, 256, 128)
    n_sh = st.tm + 2 * HALO - 8
    kern = functools.partial(_dwconv_kernel, st, width, rc, cc)
    return pl.pallas_call(
        kern,
        grid=(st.b, st.nt),
        in_specs=[st.row_spec(st.d), prev, nxt,
                  pl.BlockSpec((width, st.d), lambda b, i: (0, 0)),
                  _vec_spec(st.d), _vec_spec(st.d), _vec_spec(st.d)],
        out_specs=st.row_spec(st.d),
        out_shape=jax.ShapeDtypeStruct((st.b, st.l, st.d), BF16),
        scratch_shapes=[pltpu.VMEM((st.tm + 2 * HALO, st.d), F32), pltpu.VMEM((7, n_sh, cc), F32),
                        pltpu.VMEM((st.tm, st.d), F32)],
        compiler_params=_cp("parallel", "parallel"),
        name="dwconv_ln_silu",
    )(u, u, u, dw_w, dw_b.reshape(1, -1), ln_g.reshape(1, -1), ln_b.reshape(1, -1))


def _pool_kernel(st, x_ref, xp_ref, xn_ref, g_ref, sc_ref, sh_ref, w_ref, ps_ref, gate_ref, o_ref, win_ref):
    i = pl.program_id(1)
    valid, prev_ok, next_ok, pos, seg_len = _segment_masks(st, i)
    tm, d = st.tm, st.d
    g, sc, sh = g_ref[...], sc_ref[0], sh_ref[0]
    x = x_ref[0]
    n = jnp.where(valid, _rms_mod(x, g, sc, sh), 0.0)
    win_ref[0:HALO, :] = jnp.where(prev_ok, _rms_mod(xp_ref[0], g, sc, sh), 0.0)
    win_ref[HALO:HALO + tm, :] = n
    win_ref[HALO + tm:, :] = jnp.where(next_ok, _rms_mod(xn_ref[0], g, sc, sh), 0.0)
    grp = d // len(POOL_WINDOWS)
    for gi, w in enumerate(POOL_WINDOWS):
        c0 = gi * grp
        tot = jnp.zeros((tm, grp), F32)
        for j in range(-(w // 2), w - w // 2):
            tot = tot + win_ref[HALO + j:HALO + j + tm, c0:c0 + grp]
        cnt = jnp.minimum(pos - w // 2 + w, seg_len) - jnp.maximum(pos - w // 2, 0)
        cnt = jnp.maximum(cnt, 1).astype(F32)
        diff = (tot / cnt - n[:, c0:c0 + grp]).astype(BF16)
        y = jnp.dot(diff, w_ref[gi], preferred_element_type=F32) * ps_ref[:, c0:c0 + grp]
        o_ref[0, :, c0:c0 + grp] = x[:, c0:c0 + grp] + gate_ref[0][:, c0:c0 + grp] * y


def _pool_call(st, x, g, sc, sh, pool_w, pool_scale, gate):
    prev, nxt = _halo_specs(st)
    ng, grp, _ = pool_w.shape
    return pl.pallas_call(
        functools.partial(_pool_kernel, st),
        grid=(st.b, st.nt),
        in_specs=[st.row_spec(st.d), prev, nxt, _vec_spec(st.d), st.mod_spec(), st.mod_spec(),
                  pl.BlockSpec((ng, grp, grp), lambda b, i: (0, 0, 0)), _vec_spec(st.d), st.mod_spec()],
        out_specs=st.row_spec(st.d),
        out_shape=jax.ShapeDtypeStruct((st.b, st.l, st.d), F32),
        scratch_shapes=[pltpu.VMEM((st.tm + 2 * HALO, st.d), F32)],
        compiler_params=_cp("parallel", "parallel"),
        name="pool_mixer",
    )(x, x, x, g.reshape(1, -1), sc, sh, pool_w, pool_scale.reshape(1, -1), gate)


def _rope_tables(st, rot_dim):
    quarter = rot_dim // 4
    rows = st.s // GRID_W
    inv = ROPE_THETA ** (-jnp.arange(quarter, dtype=F32) / quarter)
    row_ang = jnp.arange(rows, dtype=F32)[:, None, None] * inv
    col_ang = jnp.arange(GRID_W, dtype=F32)[None, :, None] * inv
    ang = jnp.concatenate([jnp.broadcast_to(row_ang, (rows, GRID_W, quarter)),
                           jnp.broadcast_to(col_ang, (rows, GRID_W, quarter))], axis=-1).reshape(st.s, 2 * quarter)
    cos, sin = jnp.cos(ang), jnp.sin(ang)
    c_tab = jnp.concatenate([cos, cos], axis=-1)
    s_tab = jnp.concatenate([-sin, sin], axis=-1)
    c_tab = jnp.concatenate([c_tab, jnp.ones((st.ctx, rot_dim), F32)], axis=0)
    s_tab = jnp.concatenate([s_tab, jnp.zeros((st.ctx, rot_dim), F32)], axis=0)
    return c_tab, s_tab


def _tab_spec(st, width):
    return pl.BlockSpec((st.tm, width), lambda b, i, *_: (i, 0))


def _folded_rope(r, cs):
    t = r * cs
    t = t + pltpu.roll(t, 64, axis=1)
    lane = lax.broadcasted_iota(jnp.int32, t.shape, 1)
    return jnp.where(lane < MLA_ROPE, t, 0.0)


def _mla_down_kernel(q_rank, kv_rank, n_ref, w_ref, qn_ref, kvn_ref, cs_ref, cq_ref, ckv_ref, kr_ref):
    acc = jnp.dot(n_ref[0], w_ref[...], preferred_element_type=F32)
    cq = acc[:, :q_rank]
    cq_ref[0] = (cq * lax.rsqrt(jnp.mean(cq * cq, axis=-1, keepdims=True) + EPS) * qn_ref[...]).astype(BF16)
    ckv = acc[:, q_rank:q_rank + kv_rank]
    ckv_ref[0] = (ckv * lax.rsqrt(jnp.mean(ckv * ckv, axis=-1, keepdims=True) + EPS) * kvn_ref[...]).astype(BF16)
    kr_ref[0] = _folded_rope(acc[:, q_rank + kv_rank:], cs_ref[...]).astype(BF16)


def _mla_down_call(st, n, w_cat, q_norm, kv_norm, cs_tab):
    k, ncols = w_cat.shape
    q_rank, kv_rank = q_norm.shape[0], kv_norm.shape[0]
    assert ncols == q_rank + kv_rank + 128 and q_rank % 128 == 0 and kv_rank % 128 == 0
    return pl.pallas_call(
        functools.partial(_mla_down_kernel, q_rank, kv_rank),
        grid=(st.b, st.nt),
        in_specs=[st.row_spec(k), pl.BlockSpec((k, ncols), lambda b, i: (0, 0)),
                  _vec_spec(q_rank), _vec_spec(kv_rank), _tab_spec(st, 128)],
        out_specs=[st.row_spec(q_rank), st.row_spec(kv_rank), st.row_spec(128)],
        out_shape=[jax.ShapeDtypeStruct((st.b, st.l, q_rank), BF16),
                   jax.ShapeDtypeStruct((st.b, st.l, kv_rank), BF16),
                   jax.ShapeDtypeStruct((st.b, st.l, 128), BF16)],
        compiler_params=_cp("parallel", "parallel"),
        name="mla_down",
    )(n, w_cat, q_norm.reshape(1, -1), kv_norm.reshape(1, -1), cs_tab)


def _mla_uq_kernel(heads, qscale, cq_ref, w_ref, cs_ref, qt_ref):
    cq, cs = cq_ref[0], cs_ref[...]
    for h in range(heads):
        acc = jnp.dot(cq, w_ref[:, h * MLA_QK_PAD:(h + 1) * MLA_QK_PAD], preferred_element_type=F32)
        q = jnp.concatenate([acc[:, :MLA_NOPE], _folded_rope(acc[:, MLA_NOPE:], cs)], axis=1) * qscale
        qt_ref[0, h] = q.T.astype(BF16)


def _mla_uq_call(st, cq, w_ext, cs_tab, heads, qscale):
    k = cq.shape[-1]
    return pl.pallas_call(
        functools.partial(_mla_uq_kernel, heads, qscale),
        grid=(st.b, st.nt),
        in_specs=[st.row_spec(k), pl.BlockSpec((k, heads * MLA_QK_PAD), lambda b, i: (0, 0)), _tab_spec(st, 128)],
        out_specs=pl.BlockSpec((1, heads, MLA_QK_PAD, st.tm), lambda b, i: (b, 0, 0, i)),
        out_shape=jax.ShapeDtypeStruct((st.b, heads, MLA_QK_PAD, st.l), BF16),
        compiler_params=_cp("parallel", "parallel"),
        name="mla_up_q",
    )(cq, w_ext, cs_tab)


def _mla_ukv_kernel(heads, ckv_ref, w_ref, kr_ref, k_ref, vt_ref):
    ckv, kr = ckv_ref[0], kr_ref[0]
    hw = MLA_NOPE + MLA_V
    for h in range(heads):
        acc = jnp.dot(ckv, w_ref[:, h * hw:(h + 1) * hw], preferred_element_type=F32)
        k_ref[0, h, :, :MLA_NOPE] = acc[:, :MLA_NOPE].astype(BF16)
        k_ref[0, h, :, MLA_NOPE:] = kr
        vt_ref[0, h] = acc[:, MLA_NOPE:].T.astype(BF16)


def _mla_ukv_call(st, ckv, w_ukv, kr, heads):
    k = ckv.shape[-1]
    hw = MLA_NOPE + MLA_V
    return pl.pallas_call(
        functools.partial(_mla_ukv_kernel, heads),
        grid=(st.b, st.nt),
        in_specs=[st.row_spec(k), pl.BlockSpec((k, heads * hw), lambda b, i: (0, 0)), st.row_spec(128)],
        out_specs=[pl.BlockSpec((1, heads, st.tm, MLA_QK_PAD), lambda b, i: (b, 0, i, 0)),
                   pl.BlockSpec((1, heads, MLA_V, st.tm), lambda b, i: (b, 0, 0, i))],
        out_shape=[jax.ShapeDtypeStruct((st.b, heads, st.l, MLA_QK_PAD), BF16),
                   jax.ShapeDtypeStruct((st.b, heads, MLA_V, st.l), BF16)],
        compiler_params=_cp("parallel", "parallel"),
        name="mla_up_kv",
    )(ckv, w_ukv, kr)


def _softmax_chunk_t(st, tk, s_ref, j, vt, m_ref, l_ref, acc_ref, masked):
    s = s_ref[...]
    if masked:
        row = j * tk + lax.broadcasted_iota(jnp.int32, s.shape, 0)
        s = jnp.where(row >= st.s, s, NEG_BIG)
    m_prev = m_ref[...]
    m_new = jnp.maximum(m_prev, jnp.max(s, axis=0, keepdims=True))
    alpha = jnp.exp2(m_prev - m_new)
    p = jnp.exp2(s - m_new)
    l_ref[...] = alpha * l_ref[...] + jnp.sum(p, axis=0, keepdims=True)
    acc_ref[...] = alpha * acc_ref[...] + jnp.dot(vt, p.astype(BF16), preferred_element_type=F32)
    m_ref[...] = m_new


def _pipelined_key_loop(st, tq, tk, qi, qk, proc, bufs0, bufs1):
    n_lat_q = st.s // tq
    nch = st.l // tk

    @pl.when(qi < n_lat_q)
    def _():
        qk(0, bufs0)

        def pair(jj, carry):
            qk(2 * jj + 1, bufs1)
            proc(2 * jj, bufs0, False)
            qk(2 * jj + 2, bufs0)
            proc(2 * jj + 1, bufs1, False)
            return carry

        lax.fori_loop(0, (nch - 1) // 2, pair, 0)
        if nch % 2 == 1:
            proc(nch - 1, bufs0, False)
        else:
            qk(nch - 1, bufs1)
            proc(nch - 2, bufs0, False)
            proc(nch - 1, bufs1, False)

    @pl.when(qi == n_lat_q)
    def _():
        for j in range(st.s // tk, nch):
            qk(j, bufs0)
            proc(j, bufs0, True)


def _chunk(j, tk):
    return pl.ds(pl.multiple_of(j * tk, tk), tk)


def _mla_attn_kernel(st, tq, tk, qt_ref, k_ref, vt_ref, o_ref, s0_ref, s1_ref, m_ref, l_ref, acc_ref):
    m_ref[...] = jnp.full_like(m_ref, NEG_BIG)
    l_ref[...] = jnp.zeros_like(l_ref)
    acc_ref[...] = jnp.zeros_like(acc_ref)

    def qk(j, s_ref):
        s_ref[...] = jnp.dot(k_ref[0, 0, _chunk(j, tk), :], qt_ref[0, 0], preferred_element_type=F32)

    def proc(j, s_ref, masked):
        _softmax_chunk_t(st, tk, s_ref, j, vt_ref[0, 0, :, _chunk(j, tk)], m_ref, l_ref, acc_ref, masked)

    _pipelined_key_loop(st, tq, tk, pl.program_id(2), qk, proc, s0_ref, s1_ref)
    o_ref[0] = (acc_ref[...] / l_ref[...]).T.astype(o_ref.dtype)


def _attn_tiles(st, tq_target):
    tq = _divisor(st.s, tq_target, 128)
    assert st.ctx <= tq
    tk = _divisor(st.l, 1536, 256 if st.l % 256 == 0 else 128)
    return tq, tk


def _mla_attn_call(st, qt, k, vt, heads):
    tq, tk = _attn_tiles(st, 1024)
    return pl.pallas_call(
        functools.partial(_mla_attn_kernel, st, tq, tk),
        grid=(st.b, heads, st.s // tq + 1),
        in_specs=[pl.BlockSpec((1, 1, MLA_QK_PAD, tq), lambda b, h, i: (b, h, 0, i)),
                  pl.BlockSpec((1, 1, st.l, MLA_QK_PAD), lambda b, h, i: (b, h, 0, 0)),
                  pl.BlockSpec((1, 1, MLA_V, st.l), lambda b, h, i: (b, h, 0, 0))],
        out_specs=pl.BlockSpec((1, tq, MLA_V), lambda b, h, i: (b, i, h)),
        out_shape=jax.ShapeDtypeStruct((st.b, st.l, heads * MLA_V), BF16),
        scratch_shapes=[pltpu.VMEM((tk, tq), F32), pltpu.VMEM((tk, tq), F32),
                        pltpu.VMEM((1, tq), F32), pltpu.VMEM((1, tq), F32), pltpu.VMEM((MLA_V, tq), F32)],
        compiler_params=_cp("parallel", "parallel", "arbitrary"),
        name="mla_attention",
    )(qt, k, vt)


def _diff_proj_kernel(heads, mode, qscale, n_ref, w_ref, c_ref, s_ref, o_ref):
    hd = DIFF_HEAD_DIM
    acc = jnp.dot(n_ref[0], w_ref[...], preferred_element_type=F32)
    if mode != "v":
        c, s = c_ref[...], s_ref[...]
    for h in range(heads):
        if mode == "v":
            o_ref[0, h] = acc[:, 2 * h * hd:2 * (h + 1) * hd].T.astype(BF16)
            continue
        halves = []
        for n in range(2):
            xh = acc[:, (2 * h + n) * hd:(2 * h + n + 1) * hd]
            halves.append(xh * c + pltpu.roll(xh, hd // 2, axis=1) * s)
        if mode == "k":
            o_ref[0, :, 2 * h * hd:(2 * h + 1) * hd] = halves[0].astype(BF16)
            o_ref[0, :, (2 * h + 1) * hd:2 * (h + 1) * hd] = halves[1].astype(BF16)
        else:
            o_ref[0, h] = (jnp.concatenate(halves, axis=1) * qscale).T.astype(BF16)


def _diff_proj_call(st, n, w_qkv, c_tab, s_tab, heads, mode, qscale=1.0):
    k = w_qkv.shape[0]
    hw = 2 * DIFF_HEAD_DIM
    third = {"q": 0, "k": 1, "v": 2}[mode]
    if mode == "k":
        out_spec = st.row_spec(heads * hw)
        out_shape = jax.ShapeDtypeStruct((st.b, st.l, heads * hw), BF16)
    else:
        out_spec = pl.BlockSpec((1, heads, hw, st.tm), lambda b, i: (b, 0, 0, i))
        out_shape = jax.ShapeDtypeStruct((st.b, heads, hw, st.l), BF16)
    return pl.pallas_call(
        functools.partial(_diff_proj_kernel, heads, mode, qscale),
        grid=(st.b, st.nt),
        in_specs=[st.row_spec(k), pl.BlockSpec((k, heads * hw), lambda b, i: (0, third)),
                  _tab_spec(st, DIFF_HEAD_DIM), _tab_spec(st, DIFF_HEAD_DIM)],
        out_specs=out_spec,
        out_shape=out_shape,
        compiler_params=_cp("parallel", "parallel"),
        name="diff_proj_" + mode,
    )(n, w_qkv, c_tab, s_tab)


def _diff_attn_kernel(st, tq, tk, lam_init, qt_ref, k_ref, vt_ref, lq1_ref, lk1_ref, lq2_ref, lk2_ref, sg_ref, o_ref,
                      sa0, sb0, sa1, sb1, m1, l1, a1, m2, l2, a2):
    hd = DIFF_HEAD_DIM
    for m_ref, l_ref, a_ref in ((m1, l1, a1), (m2, l2, a2)):
        m_ref[...] = jnp.full_like(m_ref, NEG_BIG)
        l_ref[...] = jnp.zeros_like(l_ref)
        a_ref[...] = jnp.zeros_like(a_ref)

    def qk(j, bufs):
        kc = k_ref[0, _chunk(j, tk), :]
        bufs[0][...] = jnp.dot(kc[:, :hd], qt_ref[0, 0, :hd, :], preferred_element_type=F32)
        bufs[1][...] = jnp.dot(kc[:, hd:], qt_ref[0, 0, hd:, :], preferred_element_type=F32)

    def proc(j, bufs, masked):
        vt = vt_ref[0, 0, :, _chunk(j, tk)]
        _softmax_chunk_t(st, tk, bufs[0], j, vt, m1, l1, a1, masked)
        _softmax_chunk_t(st, tk, bufs[1], j, vt, m2, l2, a2, masked)

    _pipelined_key_loop(st, tq, tk, pl.program_id(2), qk, proc, (sa0, sb0), (sa1, sb1))
    lam = (jnp.exp(jnp.sum(lq1_ref[...] * lk1_ref[...], axis=-1, keepdims=True))
           - jnp.exp(jnp.sum(lq2_ref[...] * lk2_ref[...], axis=-1, keepdims=True)) + lam_init)
    o = a1[...] / l1[...] - lam * (a2[...] / l2[...])
    o = o * lax.rsqrt(jnp.mean(o * o, axis=0, keepdims=True) + EPS) * sg_ref[...]
    o_ref[0] = (o * (1.0 - lam_init)).T.astype(o_ref.dtype)


def _diff_attn_call(st, qt, k, vt, lq1, lk1, lq2, lk2, subln_g, heads, lam_init):
    tq, tk = _attn_tiles(st, 512)
    hw = 2 * DIFF_HEAD_DIM
    vec = lambda a: a.reshape(1, -1)
    return pl.pallas_call(
        functools.partial(_diff_attn_kernel, st, tq, tk, lam_init),
        grid=(st.b, heads, st.s // tq + 1),
        in_specs=[pl.BlockSpec((1, 1, hw, tq), lambda b, h, i: (b, h, 0, i)),
                  pl.BlockSpec((1, st.l, hw), lambda b, h, i: (b, 0, h)),
                  pl.BlockSpec((1, 1, hw, st.l), lambda b, h, i: (b, h, 0, 0)),
                  _vec_spec(DIFF_HEAD_DIM), _vec_spec(DIFF_HEAD_DIM), _vec_spec(DIFF_HEAD_DIM),
                  _vec_spec(DIFF_HEAD_DIM), pl.BlockSpec((hw, 1), lambda *_: (0, 0))],
        out_specs=pl.BlockSpec((1, tq, hw), lambda b, h, i: (b, i, h)),
        out_shape=jax.ShapeDtypeStruct((st.b, st.l, heads * hw), BF16),
        scratch_shapes=[pltpu.VMEM((tk, tq), F32)] * 4
        + [pltpu.VMEM((1, tq), F32), pltpu.VMEM((1, tq), F32), pltpu.VMEM((hw, tq), F32)] * 2,
        compiler_params=_cp("parallel", "parallel", "arbitrary"),
        name="diff_attention",
    )(qt, k, vt, vec(lq1), vec(lk1), vec(lq2), vec(lk2), subln_g.reshape(-1, 1))


def _expert_kernel(tm, n_steps, idx_hbm, x_ref, wg_ref, wu_ref, wd_ref, g_ref, gate_ref, acc_in, acc_hbm,
                   idx_smem, rows_ref, sem_idx, sem_gather, sem_scatter):
    del acc_in
    first_of_expert = pl.program_id(1) == 0
    k = pl.program_id(0) * pl.num_programs(1) + pl.program_id(1)
    slot = lax.rem(k, 2)

    def idx_copy(step, s):
        return pltpu.make_async_copy(idx_hbm.at[step], idx_smem.at[s], sem_idx.at[s])

    def row_copy(r, sem, to_hbm):
        hbm, vmem = acc_hbm.at[pl.ds(idx_smem[slot, r], 1), :], rows_ref.at[slot, pl.ds(r, 1), :]
        return pltpu.make_async_copy(vmem, hbm, sem) if to_hbm else pltpu.make_async_copy(hbm, vmem, sem)

    def all_rows(sem):
        return pltpu.make_async_copy(acc_hbm.at[pl.ds(0, tm), :], rows_ref.at[0], sem)

    @pl.when(k == 0)
    def _():
        idx_copy(0, 0).start()

    idx_copy(k, slot).wait()

    @pl.when(k + 1 < n_steps)
    def _():
        idx_copy(k + 1, 1 - slot).start()

    @pl.when(jnp.logical_and(k > 0, first_of_expert))
    def _():
        all_rows(sem_scatter).wait()

    for r in range(tm):
        row_copy(r, sem_gather, False).start()

    x = x_ref[0]
    a = jnp.dot(x, wg_ref[0], preferred_element_type=F32)
    u = jnp.dot(x, wu_ref[0], preferred_element_type=F32)
    h = (_silu(a) * u).astype(BF16)
    y = jnp.dot(h, wd_ref[0], preferred_element_type=F32) * g_ref[0]

    all_rows(sem_gather).wait()

    @pl.when(jnp.logical_not(first_of_expert))
    def _():
        all_rows(sem_scatter).wait()

    rows_ref[slot] = rows_ref[slot] + gate_ref[0] * y
    for r in range(tm):
        row_copy(r, sem_scatter, True).start()

    @pl.when(k == n_steps - 1)
    def _():
        all_rows(sem_scatter).wait()


def _expert_call(xs, wg, wu, wd, gates, idx, acc, gate, tm, gate_row0):
    n_e, rows, d = xs.shape
    ff = wg.shape[-1]
    nt = rows // tm
    tiles_per_sample = nt // (gate.shape[0] // 2)
    params = pltpu.CompilerParams(dimension_semantics=("arbitrary", "arbitrary"), vmem_limit_bytes=VMEM_LIMIT_BYTES,
                                  disable_bounds_checks=True)
    return pl.pallas_call(
        functools.partial(_expert_kernel, tm, n_e * nt),
        grid=(n_e, nt),
        in_specs=[pl.BlockSpec(memory_space=pl.ANY),
                  pl.BlockSpec((1, tm, d), lambda e, i: (e, i, 0)),
                  pl.BlockSpec((1, d, ff), lambda e, i: (e, 0, 0)),
                  pl.BlockSpec((1, d, ff), lambda e, i: (e, 0, 0)),
                  pl.BlockSpec((1, ff, d), lambda e, i: (e, 0, 0)),
                  pl.BlockSpec((1, tm, 1), lambda e, i: (e, i, 0)),
                  pl.BlockSpec((1, 1, d), lambda e, i: ((i // tiles_per_sample) * 2 + gate_row0, 0, 0)),
                  pl.BlockSpec(memory_space=pl.ANY)],
        out_specs=pl.BlockSpec(memory_space=pl.ANY),
        out_shape=jax.ShapeDtypeStruct(acc.shape, F32),
        input_output_aliases={7: 0},
        scratch_shapes=[pltpu.SMEM((2, tm), jnp.int32), pltpu.VMEM((2, tm, d), F32),
                        pltpu.SemaphoreType.DMA((2,)), pltpu.SemaphoreType.DMA, pltpu.SemaphoreType.DMA],
        compiler_params=params,
        name="moe_experts",
    )(idx.reshape(n_e * nt, tm), xs, wg, wu, wd, gates, gate, acc)


def _route(st, m, aff_seg, offset, cap):
    n_e = aff_seg.shape[1]
    g_sel, idx = lax.top_k(aff_seg, cap)
    idx = idx + offset
    bidx = jnp.arange(st.b)[:, None, None]
    xs = jnp.swapaxes(m[bidx, idx], 0, 1).reshape(n_e, st.b * cap, st.d)
    gates = jnp.swapaxes(g_sel, 0, 1).reshape(n_e, st.b * cap, 1)
    rows = jnp.swapaxes(idx + bidx * st.l, 0, 1).reshape(n_e, st.b * cap).astype(jnp.int32)
    return xs, gates, rows


def _moe(st, x, g, sc, sh, gate, router_w, wg, wu, wd, with_ctx):
    n_e = router_w.shape[-1]
    m, aff = _norm_router_call(st, x, g, sc, sh, router_w.T.astype(BF16))
    wg, wu, wd = wg.astype(BF16), wu.astype(BF16), wd.astype(BF16)
    acc = x.reshape(st.b * st.l, st.d)
    cap_l = max(1, EC_FACTOR * st.s // n_e)
    xs, gates, rows = _route(st, m, aff[:, :, :st.s], 0, cap_l)
    acc = _expert_call(xs, wg, wu, wd, gates, rows, acc, gate, _divisor(cap_l, 256, 16), 0)
    if with_ctx:
        cap_c = max(1, EC_FACTOR * st.ctx // n_e)
        xs, gates, rows = _route(st, m, aff[:, :, st.s:], st.s, cap_c)
        acc = _expert_call(xs, wg, wu, wd, gates, rows, acc, gate, _divisor(cap_c, 256, 16), 1)
    return acc.reshape(st.b, st.l, st.d)


def _swap_halves_cols(w, width):
    k, n = w.shape
    w = w.reshape(k, n // width, 2, width // 2)
    return w[:, :, ::-1, :].reshape(k, n)


def kernel(x, c, ctx, c_ctx, ada_w, ada_b, norm_g, final_g, conv_pw1_w, conv_pw1_b, conv_dw_w, conv_dw_b, conv_ln_g, conv_ln_b, conv_pw2_w, conv_pw2_b, mla_w_dq, mla_q_norm, mla_w_uq, mla_w_dkv, mla_kv_norm, mla_w_ukv, mla_w_o, pool_w, pool_scale, diff_w_qkv, diff_lq1, diff_lk1, diff_lq2, diff_lk2, diff_subln_g, diff_w_o, moe_router, moe_w_gate, moe_w_up, moe_w_down):
    b, s, d = x.shape
    n_ctx = ctx.shape[1]
    depth = ada_w.shape[0]
    st = _Stream(b, s, n_ctx, d, _divisor(s, 512, max(n_ctx, 128)))
    xs = jnp.concatenate([x, ctx], axis=1)

    c8 = jnp.zeros((8, d), F32).at[:b].set(c).at[b].set(c_ctx)
    mod = _mod_call(c8, ada_w, ada_b)
    order = jnp.stack([jnp.arange(b), jnp.full((b,), b)], axis=1).reshape(-1)
    mod = mod[:, order, :].reshape(depth, 2 * b, 6, d)
    mods = [[mod[i, :, k, :].reshape(2 * b, 1, d) for k in range(6)] for i in range(depth)]

    heads = mla_w_uq.shape[1] // (MLA_NOPE + MLA_ROPE)
    q_rank, kv_rank = mla_q_norm.shape[0], mla_kv_norm.shape[0]
    diff_heads = diff_w_o.shape[0] // (2 * DIFF_HEAD_DIM)
    zeros_d = jnp.zeros((d,), F32)

    for i in range(depth):
        kind = i % 4
        need_ctx_out = i < depth - 1
        sh1, sc1, gt1, sh2, sc2, gt2 = mods[i]
        if kind == 0:
            n = _norm_call(st, xs, norm_g[i, 0], sc1, sh1)
            u = _glu_call(st, n, conv_pw1_w.astype(BF16), conv_pw1_b)
            v = _dwconv_call(st, u, conv_dw_w, conv_dw_b, conv_ln_g, conv_ln_b)
            xs = _proj_res_call(st, v, conv_pw2_w.astype(BF16), conv_pw2_b, xs, gt1)
        elif kind == 1:
            n = _norm_call(st, xs, norm_g[i, 0], sc1, sh1)
            c_tab, s_tab = _rope_tables(st, MLA_ROPE)
            cs_tab = jnp.concatenate([c_tab, s_tab], axis=-1)
            w_kr = mla_w_dkv[:, kv_rank:]
            w_cat = jnp.concatenate([mla_w_dq, mla_w_dkv[:, :kv_rank], w_kr, _swap_halves_cols(w_kr, MLA_ROPE)],
                                    axis=1).astype(BF16)
            cq, ckv, kr = _mla_down_call(st, n, w_cat, mla_q_norm, mla_kv_norm, cs_tab)
            w_uq = mla_w_uq.reshape(q_rank, heads, MLA_NOPE + MLA_ROPE)
            w_uq_rope = w_uq[:, :, MLA_NOPE:].reshape(q_rank, heads * MLA_ROPE)
            w_uq_ext = jnp.concatenate(
                [w_uq, _swap_halves_cols(w_uq_rope, MLA_ROPE).reshape(q_rank, heads, MLA_ROPE)], axis=-1)
            qscale = (MLA_NOPE + MLA_ROPE) ** -0.5 * LOG2E
            qt = _mla_uq_call(st, cq, w_uq_ext.reshape(q_rank, heads * MLA_QK_PAD).astype(BF16), cs_tab, heads, qscale)
            k, vt = _mla_ukv_call(st, ckv, mla_w_ukv.astype(BF16), kr, heads)
            o = _mla_attn_call(st, qt, k, vt, heads)
            xs = _proj_res_call(st, o, mla_w_o.astype(BF16), zeros_d, xs, gt1)
        elif kind == 2:
            xs = _pool_call(st, xs, norm_g[i, 0], sc1, sh1, pool_w.astype(BF16), pool_scale, gt1)
        else:
            n = _norm_call(st, xs, norm_g[i, 0], sc1, sh1)
            c_tab, s_tab = _rope_tables(st, DIFF_HEAD_DIM)
            w_qkv = diff_w_qkv.astype(BF16)
            qt = _diff_proj_call(st, n, w_qkv, c_tab, s_tab, diff_heads, "q", DIFF_HEAD_DIM ** -0.5 * LOG2E)
            k = _diff_proj_call(st, n, w_qkv, c_tab, s_tab, diff_heads, "k")
            vt = _diff_proj_call(st, n, w_qkv, c_tab, s_tab, diff_heads, "v")
            lam_init = 0.8 - 0.6 * math.exp(-0.3 * i)
            o = _diff_attn_call(st, qt, k, vt, diff_lq1, diff_lk1, diff_lq2, diff_lk2, diff_subln_g, diff_heads,
                                lam_init)
            xs = _proj_res_call(st, o, diff_w_o.astype(BF16), zeros_d, xs, gt1)
        xs = _moe(st, xs, norm_g[i, 1], sc2, sh2, gt2, moe_router[i], moe_w_gate[i], moe_w_up[i], moe_w_down[i],
                  need_ctx_out)
    return _final_norm_call(st, xs, final_g)
```

```python
import functools
import math

import jax
import jax.numpy as jnp
from jax import lax
from jax.experimental import pallas as pl
from jax.experimental.pallas import tpu as pltpu

F32 = jnp.float32
BF16 = jnp.bfloat16

GRID_W = 64
EPS = 1e-6
ROPE_THETA = 10000.0
MLA_NOPE = 128
MLA_ROPE = 64
MLA_V = 128
MLA_QK_PAD = 256
DIFF_HEAD_DIM = 128
POOL_WINDOWS = (2, 4, 8, 16)
EC_FACTOR = 2
HALO = 16
NEG_BIG = -1e30
LOG2E = 1.4426950408889634

VMEM_LIMIT_BYTES = 56 * 1024 * 1024


def _cp(*sem):
    return pltpu.CompilerParams(dimension_semantics=sem, vmem_limit_bytes=VMEM_LIMIT_BYTES)


def _divisor(n, target, mult):
    best = None
    d = mult
    while d <= min(n, target):
        if n % d == 0:
            best = d
        d += mult
    return best if best is not None else n


def _silu(v):
    return v * jax.nn.sigmoid(v)


def _rms_mod(x, g, sc, sh):
    ms = jnp.mean(x * x, axis=-1, keepdims=True)
    return (x * lax.rsqrt(ms + EPS) * g) * (1.0 + sc) + sh


def _mod_kernel(c_ref, w_ref, b_ref, o_ref):
    s = _silu(c_ref[...]).astype(BF16)
    o_ref[0] = jnp.dot(s, w_ref[0].astype(BF16), preferred_element_type=F32) + b_ref[0]


def _mod_call(c8, ada_w, ada_b):
    depth, d, n = ada_w.shape
    tn = _divisor(n, 1024, 128)
    return pl.pallas_call(
        _mod_kernel,
        grid=(depth, n // tn),
        in_specs=[pl.BlockSpec((8, d), lambda l, j: (0, 0)),
                  pl.BlockSpec((1, d, tn), lambda l, j: (l, 0, j)),
                  pl.BlockSpec((1, 1, tn), lambda l, j: (l, 0, j))],
        out_specs=pl.BlockSpec((1, 8, tn), lambda l, j: (l, 0, j)),
        out_shape=jax.ShapeDtypeStruct((depth, 8, n), F32),
        compiler_params=_cp("arbitrary", "arbitrary"),
        name="adaln_mod",
    )(c8, ada_w, ada_b.reshape(depth, 1, n))


class _Stream:
    def __init__(self, b, s, ctx, d, tm):
        assert s % tm == 0 and ctx <= tm and ctx % HALO == 0 and tm % HALO == 0
        self.b, self.s, self.ctx, self.d, self.tm = b, s, ctx, d, tm
        self.l = s + ctx
        self.n_lat = s // tm
        self.nt = self.n_lat + 1

    def row_spec(self, width, col=None):
        if col is None:
            return pl.BlockSpec((1, self.tm, width), lambda b, i, *_: (b, i, 0))
        return pl.BlockSpec((1, self.tm, width), lambda b, i, *_: (b, i, col))

    def mod_spec(self):
        n_lat = self.n_lat
        return pl.BlockSpec((1, 1, self.d), lambda b, i, *_: (b * 2 + i // n_lat, 0, 0))


def _vec_spec(width):
    return pl.BlockSpec((1, width), lambda *_: (0, 0))


def _norm_kernel(x_ref, g_ref, sc_ref, sh_ref, o_ref):
    o_ref[0] = _rms_mod(x_ref[0], g_ref[...], sc_ref[0], sh_ref[0]).astype(o_ref.dtype)


def _norm_call(st, x, g, sc, sh):
    return pl.pallas_call(
        _norm_kernel,
        grid=(st.b, st.nt),
        in_specs=[st.row_spec(st.d), _vec_spec(st.d), st.mod_spec(), st.mod_spec()],
        out_specs=st.row_spec(st.d),
        out_shape=jax.ShapeDtypeStruct((st.b, st.l, st.d), BF16),
        compiler_params=_cp("parallel", "parallel"),
        name="norm_mod",
    )(x, g.reshape(1, -1), sc, sh)


def _norm_router_kernel(x_ref, g_ref, sc_ref, sh_ref, rw_ref, m_ref, aff_ref):
    m = _rms_mod(x_ref[0], g_ref[...], sc_ref[0], sh_ref[0]).astype(BF16)
    m_ref[0] = m
    logits = lax.dot_general(rw_ref[...], m, (((1,), (1,)), ((), ())), preferred_element_type=F32)
    z = logits - jnp.max(logits, axis=0, keepdims=True)
    e = jnp.exp(z)
    aff_ref[0] = e / jnp.sum(e, axis=0, keepdims=True)


def _norm_router_call(st, x, g, sc, sh, router_t):
    n_e = router_t.shape[0]
    return pl.pallas_call(
        _norm_router_kernel,
        grid=(st.b, st.nt),
        in_specs=[st.row_spec(st.d), _vec_spec(st.d), st.mod_spec(), st.mod_spec(),
                  pl.BlockSpec((n_e, st.d), lambda b, i: (0, 0))],
        out_specs=[st.row_spec(st.d), pl.BlockSpec((1, n_e, st.tm), lambda b, i: (b, 0, i))],
        out_shape=[jax.ShapeDtypeStruct((st.b, st.l, st.d), BF16),
                   jax.ShapeDtypeStruct((st.b, n_e, st.l), F32)],
        compiler_params=_cp("parallel", "parallel"),
        name="moe_norm_router",
    )(x, g.reshape(1, -1), sc, sh, router_t)


def _final_norm_kernel(x_ref, g_ref, o_ref):
    x = x_ref[0]
    ms = jnp.mean(x * x, axis=-1, keepdims=True)
    o_ref[0] = x * lax.rsqrt(ms + EPS) * g_ref[...]


def _final_norm_call(st, x, g):
    return pl.pallas_call(
        _final_norm_kernel,
        grid=(st.b, st.n_lat),
        in_specs=[st.row_spec(st.d), _vec_spec(st.d)],
        out_specs=st.row_spec(st.d),
        out_shape=jax.ShapeDtypeStruct((st.b, st.s, st.d), F32),
        compiler_params=_cp("parallel", "parallel"),
        name="final_norm",
    )(x, g.reshape(1, -1))


def _proj_res_kernel(a_ref, w_ref, bias_ref, x_ref, gate_ref, o_ref):
    y = jnp.dot(a_ref[0], w_ref[...], preferred_element_type=F32) + bias_ref[...]
    o_ref[0] = x_ref[0] + gate_ref[0] * y


def _proj_res_call(st, a, w, bias, x, gate):
    k, n = w.shape
    return pl.pallas_call(
        _proj_res_kernel,
        grid=(st.b, st.nt),
        in_specs=[st.row_spec(k), pl.BlockSpec((k, n), lambda b, i: (0, 0)), _vec_spec(n),
                  st.row_spec(n), st.mod_spec()],
        out_specs=st.row_spec(n),
        out_shape=jax.ShapeDtypeStruct((st.b, st.l, n), F32),
        compiler_params=_cp("parallel", "parallel"),
        name="proj_residual",
    )(a, w, bias.reshape(1, -1), x, gate)


def _glu_kernel(a_ref, wa_ref, wb_ref, ba_ref, bb_ref, o_ref):
    a = a_ref[0]
    u = jnp.dot(a, wa_ref[...], preferred_element_type=F32) + ba_ref[...]
    v = jnp.dot(a, wb_ref[...], preferred_element_type=F32) + bb_ref[...]
    o_ref[0] = u * jax.nn.sigmoid(v)


def _glu_call(st, a, w, bias):
    k, n2 = w.shape
    n = n2 // 2
    tn = _divisor(n, 512, 128)
    nj = n // tn
    b2 = bias.reshape(1, -1)
    return pl.pallas_call(
        _glu_kernel,
        grid=(st.b, st.nt, nj),
        in_specs=[st.row_spec(k),
                  pl.BlockSpec((k, tn), lambda b, i, j: (0, j)),
                  pl.BlockSpec((k, tn), lambda b, i, j: (0, j + nj)),
                  pl.BlockSpec((1, tn), lambda b, i, j: (0, j)),
                  pl.BlockSpec((1, tn), lambda b, i, j: (0, j + nj))],
        out_specs=pl.BlockSpec((1, st.tm, tn), lambda b, i, j: (b, i, j)),
        out_shape=jax.ShapeDtypeStruct((st.b, st.l, n), F32),
        compiler_params=_cp("parallel", "parallel", "arbitrary"),
        name="conv_pw1_glu",
    )(a, w, w, b2, b2)


def _halo_specs(st):
    r = st.tm // HALO
    last = st.l // HALO - 1
    prev = pl.BlockSpec((1, HALO, st.d), lambda b, i: (b, jnp.maximum(i * r - 1, 0), 0))
    nxt = pl.BlockSpec((1, HALO, st.d), lambda b, i: (b, jnp.minimum((i + 1) * r, last), 0))
    return prev, nxt


def _segment_masks(st, i):
    is_ctx = i == st.n_lat
    seg_len = jnp.where(is_ctx, st.ctx, st.s)
    base = jnp.where(is_ctx, 0, i * st.tm)
    rows = lax.broadcasted_iota(jnp.int32, (st.tm, 1), 0)
    pos = base + rows
    valid = pos < seg_len
    prev_ok = jnp.logical_and(i != 0, jnp.logical_not(is_ctx))
    next_ok = i < st.n_lat - 1
    return valid, prev_ok, next_ok, pos, seg_len


def _dwconv_kernel(st, width, rc, cc, u_ref, up_ref, un_ref, w_ref, b_ref, g_ref, beta_ref, o_ref,
                   win_ref, sh_ref, acc_ref):
    i = pl.program_id(1)
    valid, prev_ok, next_ok, _, _ = _segment_masks(st, i)
    tm, d = st.tm, st.d
    pad = width // 2
    win_ref[0:HALO, :] = jnp.where(prev_ok, up_ref[0], 0.0)
    win_ref[HALO:HALO + tm, :] = jnp.where(valid, u_ref[0], 0.0)
    win_ref[HALO + tm:, :] = jnp.where(next_ok, un_ref[0], 0.0)
    n_sh = sh_ref.shape[1]

    for c0 in range(0, d, cc):
        for r in range(1, 8):
            sh_ref[r - 1] = win_ref[r:r + n_sh, c0:c0 + cc]

        def row_chunk(rb, carry, c0=c0):
            r0 = pl.multiple_of(rb * rc, rc)
            acc = jnp.zeros((rc, cc), F32)
            for k in range(width):
                off = HALO - pad + k
                a8, res = 8 * (off // 8), off % 8
                if res == 0:
                    tap = win_ref[pl.ds(r0 + a8, rc), c0:c0 + cc]
                else:
                    tap = sh_ref[res - 1, pl.ds(r0 + a8, rc), :]
                acc = acc + tap * w_ref[k:k + 1, c0:c0 + cc]
            acc_ref[pl.ds(r0, rc), c0:c0 + cc] = acc + b_ref[:, c0:c0 + cc]
            return carry

        lax.fori_loop(0, tm // rc, row_chunk, 0)
    y = acc_ref[...]
    mu = jnp.mean(y, axis=-1, keepdims=True)
    yc = y - mu
    var = jnp.mean(yc * yc, axis=-1, keepdims=True)
    z = yc * lax.rsqrt(var + EPS) * g_ref[...] + beta_ref[...]
    o_ref[0] = _silu(z).astype(o_ref.dtype)


def _dwconv_call(st, u, dw_w, dw_b, ln_g, ln_b):
    width = dw_w.shape[0]
    assert width // 2 < HALO
    prev, nxt = _halo_specs(st)
    rc = 32
    cc = _divisor(st.d, 256, 128)
    n_sh = st.tm + 2 * HALO - 8
    kern = functools.partial(_dwconv_kernel, st, width, rc, cc)
    return pl.pallas_call(
        kern,
        grid=(st.b, st.nt),
        in_specs=[st.row_spec(st.d), prev, nxt,
                  pl.BlockSpec((width, st.d), lambda b, i: (0, 0)),
                  _vec_spec(st.d), _vec_spec(st.d), _vec_spec(st.d)],
        out_specs=st.row_spec(st.d),
        out_shape=jax.ShapeDtypeStruct((st.b, st.l, st.d), BF16),
        scratch_shapes=[pltpu.VMEM((st.tm + 2 * HALO, st.d), F32), pltpu.VMEM((7, n_sh, cc), F32),
                        pltpu.VMEM((st.tm, st.d), F32)],
        compiler_params=_cp("parallel", "parallel"),
        name="dwconv_ln_silu",
    )(u, u, u, dw_w, dw_b.reshape(1, -1), ln_g.reshape(1, -1), ln_b.reshape(1, -1))


def _pool_kernel(st, x_ref, xp_ref, xn_ref, g_ref, sc_ref, sh_ref, w_ref, ps_ref, gate_ref, o_ref, win_ref):
    i = pl.program_id(1)
    valid, prev_ok, next_ok, pos, seg_len = _segment_masks(st, i)
    tm, d = st.tm, st.d
    g, sc, sh = g_ref[...], sc_ref[0], sh_ref[0]
    x = x_ref[0]
    n = jnp.where(valid, _rms_mod(x, g, sc, sh), 0.0)
    win_ref[0:HALO, :] = jnp.where(prev_ok, _rms_mod(xp_ref[0], g, sc, sh), 0.0)
    win_ref[HALO:HALO + tm, :] = n
    win_ref[HALO + tm:, :] = jnp.where(next_ok, _rms_mod(xn_ref[0], g, sc, sh), 0.0)
    grp = d // len(POOL_WINDOWS)
    for gi, w in enumerate(POOL_WINDOWS):
        c0 = gi * grp
        tot = jnp.zeros((tm, grp), F32)
        for j in range(-(w // 2), w - w // 2):
            tot = tot + win_ref[HALO + j:HALO + j + tm, c0:c0 + grp]
        cnt = jnp.minimum(pos - w // 2 + w, seg_len) - jnp.maximum(pos - w // 2, 0)
        cnt = jnp.maximum(cnt, 1).astype(F32)
        diff = (tot / cnt - n[:, c0:c0 + grp]).astype(BF16)
        y = jnp.dot(diff, w_ref[gi], preferred_element_type=F32) * ps_ref[:, c0:c0 + grp]
        o_ref[0, :, c0:c0 + grp] = x[:, c0:c0 + grp] + gate_ref[0][:, c0:c0 + grp] * y


def _pool_call(st, x, g, sc, sh, pool_w, pool_scale, gate):
    prev, nxt = _halo_specs(st)
    ng, grp, _ = pool_w.shape
    return pl.pallas_call(
        functools.partial(_pool_kernel, st),
        grid=(st.b, st.nt),
        in_specs=[st.row_spec(st.d), prev, nxt, _vec_spec(st.d), st.mod_spec(), st.mod_spec(),
                  pl.BlockSpec((ng, grp, grp), lambda b, i: (0, 0, 0)), _vec_spec(st.d), st.mod_spec()],
        out_specs=st.row_spec(st.d),
        out_shape=jax.ShapeDtypeStruct((st.b, st.l, st.d), F32),
        scratch_shapes=[pltpu.VMEM((st.tm + 2 * HALO, st.d), F32)],
        compiler_params=_cp("parallel", "parallel"),
        name="pool_mixer",
    )(x, x, x, g.reshape(1, -1), sc, sh, pool_w, pool_scale.reshape(1, -1), gate)


def _rope_tables(st, rot_dim):
    quarter = rot_dim // 4
    rows = st.s // GRID_W
    inv = ROPE_THETA ** (-jnp.arange(quarter, dtype=F32) / quarter)
    row_ang = jnp.arange(rows, dtype=F32)[:, None, None] * inv
    col_ang = jnp.arange(GRID_W, dtype=F32)[None, :, None] * inv
    ang = jnp.concatenate([jnp.broadcast_to(row_ang, (rows, GRID_W, quarter)),
                           jnp.broadcast_to(col_ang, (rows, GRID_W, quarter))], axis=-1).reshape(st.s, 2 * quarter)
    cos, sin = jnp.cos(ang), jnp.sin(ang)
    c_tab = jnp.concatenate([cos, cos], axis=-1)
    s_tab = jnp.concatenate([-sin, sin], axis=-1)
    c_tab = jnp.concatenate([c_tab, jnp.ones((st.ctx, rot_dim), F32)], axis=0)
    s_tab = jnp.concatenate([s_tab, jnp.zeros((st.ctx, rot_dim), F32)], axis=0)
    return c_tab, s_tab


def _tab_spec(st, width):
    return pl.BlockSpec((st.tm, width), lambda b, i, *_: (i, 0))


def _folded_rope(r, cs):
    t = r * cs
    t = t + pltpu.roll(t, 64, axis=1)
    lane = lax.broadcasted_iota(jnp.int32, t.shape, 1)
    return jnp.where(lane < MLA_ROPE, t, 0.0)


def _mla_down_kernel(q_rank, kv_rank, n_ref, w_ref, qn_ref, kvn_ref, cs_ref, cq_ref, ckv_ref, kr_ref):
    acc = jnp.dot(n_ref[0], w_ref[...], preferred_element_type=F32)
    cq = acc[:, :q_rank]
    cq_ref[0] = (cq * lax.rsqrt(jnp.mean(cq * cq, axis=-1, keepdims=True) + EPS) * qn_ref[...]).astype(BF16)
    ckv = acc[:, q_rank:q_rank + kv_rank]
    ckv_ref[0] = (ckv * lax.rsqrt(jnp.mean(ckv * ckv, axis=-1, keepdims=True) + EPS) * kvn_ref[...]).astype(BF16)
    kr_ref[0] = _folded_rope(acc[:, q_rank + kv_rank:], cs_ref[...]).astype(BF16)


def _mla_down_call(st, n, w_cat, q_norm, kv_norm, cs_tab):
    k, ncols = w_cat.shape
    q_rank, kv_rank = q_norm.shape[0], kv_norm.shape[0]
    assert ncols == q_rank + kv_rank + 128 and q_rank % 128 == 0 and kv_rank % 128 == 0
    return pl.pallas_call(
        functools.partial(_mla_down_kernel, q_rank, kv_rank),
        grid=(st.b, st.nt),
        in_specs=[st.row_spec(k), pl.BlockSpec((k, ncols), lambda b, i: (0, 0)),
                  _vec_spec(q_rank), _vec_spec(kv_rank), _tab_spec(st, 128)],
        out_specs=[st.row_spec(q_rank), st.row_spec(kv_rank), st.row_spec(128)],
        out_shape=[jax.ShapeDtypeStruct((st.b, st.l, q_rank), BF16),
                   jax.ShapeDtypeStruct((st.b, st.l, kv_rank), BF16),
                   jax.ShapeDtypeStruct((st.b, st.l, 128), BF16)],
        compiler_params=_cp("parallel", "parallel"),
        name="mla_down",
    )(n, w_cat, q_norm.reshape(1, -1), kv_norm.reshape(1, -1), cs_tab)


def _mla_uq_kernel(heads, qscale, cq_ref, w_ref, cs_ref, qt_ref):
    cq, cs = cq_ref[0], cs_ref[...]
    for h in range(heads):
        acc = jnp.dot(cq, w_ref[:, h * MLA_QK_PAD:(h + 1) * MLA_QK_PAD], preferred_element_type=F32)
        q = jnp.concatenate([acc[:, :MLA_NOPE], _folded_rope(acc[:, MLA_NOPE:], cs)], axis=1) * qscale
        qt_ref[0, h] = q.T.astype(BF16)


def _mla_uq_call(st, cq, w_ext, cs_tab, heads, qscale):
    k = cq.shape[-1]
    return pl.pallas_call(
        functools.partial(_mla_uq_kernel, heads, qscale),
        grid=(st.b, st.nt),
        in_specs=[st.row_spec(k), pl.BlockSpec((k, heads * MLA_QK_PAD), lambda b, i: (0, 0)), _tab_spec(st, 128)],
        out_specs=pl.BlockSpec((1, heads, MLA_QK_PAD, st.tm), lambda b, i: (b, 0, 0, i)),
        out_shape=jax.ShapeDtypeStruct((st.b, heads, MLA_QK_PAD, st.l), BF16),
        compiler_params=_cp("parallel", "parallel"),
        name="mla_up_q",
    )(cq, w_ext, cs_tab)


def _mla_ukv_kernel(heads, ckv_ref, w_ref, kr_ref, k_ref, vt_ref):
    ckv, kr = ckv_ref[0], kr_ref[0]
    hw = MLA_NOPE + MLA_V
    for h in range(heads):
        acc = jnp.dot(ckv, w_ref[:, h * hw:(h + 1) * hw], preferred_element_type=F32)
        k_ref[0, h, :, :MLA_NOPE] = acc[:, :MLA_NOPE].astype(BF16)
        k_ref[0, h, :, MLA_NOPE:] = kr
        vt_ref[0, h] = acc[:, MLA_NOPE:].T.astype(BF16)


def _mla_ukv_call(st, ckv, w_ukv, kr, heads):
    k = ckv.shape[-1]
    hw = MLA_NOPE + MLA_V
    return pl.pallas_call(
        functools.partial(_mla_ukv_kernel, heads),
        grid=(st.b, st.nt),
        in_specs=[st.row_spec(k), pl.BlockSpec((k, heads * hw), lambda b, i: (0, 0)), st.row_spec(128)],
        out_specs=[pl.BlockSpec((1, heads, st.tm, MLA_QK_PAD), lambda b, i: (b, 0, i, 0)),
                   pl.BlockSpec((1, heads, MLA_V, st.tm), lambda b, i: (b, 0, 0, i))],
        out_shape=[jax.ShapeDtypeStruct((st.b, heads, st.l, MLA_QK_PAD), BF16),
                   jax.ShapeDtypeStruct((st.b, heads, MLA_V, st.l), BF16)],
        compiler_params=_cp("parallel", "parallel"),
        name="mla_up_kv",
    )(ckv, w_ukv, kr)


def _softmax_chunk_t(st, tk, s_ref, j, vt, m_ref, l_ref, acc_ref, masked):
    s = s_ref[...]
    if masked:
        row = j * tk + lax.broadcasted_iota(jnp.int32, s.shape, 0)
        s = jnp.where(row >= st.s, s, NEG_BIG)
    m_prev = m_ref[...]
    m_new = jnp.maximum(m_prev, jnp.max(s, axis=0, keepdims=True))
    alpha = jnp.exp2(m_prev - m_new)
    p = jnp.exp2(s - m_new)
    l_ref[...] = alpha * l_ref[...] + jnp.sum(p, axis=0, keepdims=True)
    acc_ref[...] = alpha * acc_ref[...] + jnp.dot(vt, p.astype(BF16), preferred_element_type=F32)
    m_ref[...] = m_new


def _pipelined_key_loop(st, tq, tk, qi, qk, proc, bufs0, bufs1):
    n_lat_q = st.s // tq
    nch = st.l // tk

    @pl.when(qi < n_lat_q)
    def _():
        qk(0, bufs0)

        def pair(jj, carry):
            qk(2 * jj + 1, bufs1)
            proc(2 * jj, bufs0, False)
            qk(2 * jj + 2, bufs0)
            proc(2 * jj + 1, bufs1, False)
            return carry

        lax.fori_loop(0, (nch - 1) // 2, pair, 0)
        if nch % 2 == 1:
            proc(nch - 1, bufs0, False)
        else:
            qk(nch - 1, bufs1)
            proc(nch - 2, bufs0, False)
            proc(nch - 1, bufs1, False)

    @pl.when(qi == n_lat_q)
    def _():
        for j in range(st.s // tk, nch):
            qk(j, bufs0)
            proc(j, bufs0, True)


def _chunk(j, tk):
    return pl.ds(pl.multiple_of(j * tk, tk), tk)


def _mla_attn_kernel(st, tq, tk, qt_ref, k_ref, vt_ref, o_ref, s0_ref, s1_ref, m_ref, l_ref, acc_ref):
    m_ref[...] = jnp.full_like(m_ref, NEG_BIG)
    l_ref[...] = jnp.zeros_like(l_ref)
    acc_ref[...] = jnp.zeros_like(acc_ref)

    def qk(j, s_ref):
        s_ref[...] = jnp.dot(k_ref[0, 0, _chunk(j, tk), :], qt_ref[0, 0], preferred_element_type=F32)

    def proc(j, s_ref, masked):
        _softmax_chunk_t(st, tk, s_ref, j, vt_ref[0, 0, :, _chunk(j, tk)], m_ref, l_ref, acc_ref, masked)

    _pipelined_key_loop(st, tq, tk, pl.program_id(2), qk, proc, s0_ref, s1_ref)
    o_ref[0] = (acc_ref[...] / l_ref[...]).T.astype(o_ref.dtype)


def _attn_tiles(st, tq_target):
    tq = _divisor(st.s, tq_target, 128)
    assert st.ctx <= tq
    tk = _divisor(st.l, 1536, 256 if st.l % 256 == 0 else 128)
    return tq, tk


def _mla_attn_call(st, qt, k, vt, heads):
    tq, tk = _attn_tiles(st, 1024)
    return pl.pallas_call(
        functools.partial(_mla_attn_kernel, st, tq, tk),
        grid=(st.b, heads, st.s // tq + 1),
        in_specs=[pl.BlockSpec((1, 1, MLA_QK_PAD, tq), lambda b, h, i: (b, h, 0, i)),
                  pl.BlockSpec((1, 1, st.l, MLA_QK_PAD), lambda b, h, i: (b, h, 0, 0)),
                  pl.BlockSpec((1, 1, MLA_V, st.l), lambda b, h, i: (b, h, 0, 0))],
        out_specs=pl.BlockSpec((1, tq, MLA_V), lambda b, h, i: (b, i, h)),
        out_shape=jax.ShapeDtypeStruct((st.b, st.l, heads * MLA_V), BF16),
        scratch_shapes=[pltpu.VMEM((tk, tq), F32), pltpu.VMEM((tk, tq), F32),
                        pltpu.VMEM((1, tq), F32), pltpu.VMEM((1, tq), F32), pltpu.VMEM((MLA_V, tq), F32)],
        compiler_params=_cp("parallel", "parallel", "arbitrary"),
        name="mla_attention",
    )(qt, k, vt)


def _diff_proj_kernel(heads, mode, qscale, n_ref, w_ref, c_ref, s_ref, o_ref):
    hd = DIFF_HEAD_DIM
    acc = jnp.dot(n_ref[0], w_ref[...], preferred_element_type=F32)
    if mode != "v":
        c, s = c_ref[...], s_ref[...]
    for h in range(heads):
        if mode == "v":
            o_ref[0, h] = acc[:, 2 * h * hd:2 * (h + 1) * hd].T.astype(BF16)
            continue
        halves = []
        for n in range(2):
            xh = acc[:, (2 * h + n) * hd:(2 * h + n + 1) * hd]
            halves.append(xh * c + pltpu.roll(xh, hd // 2, axis=1) * s)
        if mode == "k":
            o_ref[0, :, 2 * h * hd:(2 * h + 1) * hd] = halves[0].astype(BF16)
            o_ref[0, :, (2 * h + 1) * hd:2 * (h + 1) * hd] = halves[1].astype(BF16)
        else:
            o_ref[0, h] = (jnp.concatenate(halves, axis=1) * qscale).T.astype(BF16)


def _diff_proj_call(st, n, w_qkv, c_tab, s_tab, heads, mode, qscale=1.0):
    k = w_qkv.shape[0]
    hw = 2 * DIFF_HEAD_DIM
    third = {"q": 0, "k": 1, "v": 2}[mode]
    if mode == "k":
        out_spec = st.row_spec(heads * hw)
        out_shape = jax.ShapeDtypeStruct((st.b, st.l, heads * hw), BF16)
    else:
        out_spec = pl.BlockSpec((1, heads, hw, st.tm), lambda b, i: (b, 0, 0, i))
        out_shape = jax.ShapeDtypeStruct((st.b, heads, hw, st.l), BF16)
    return pl.pallas_call(
        functools.partial(_diff_proj_kernel, heads, mode, qscale),
        grid=(st.b, st.nt),
        in_specs=[st.row_spec(k), pl.BlockSpec((k, heads * hw), lambda b, i: (0, third)),
                  _tab_spec(st, DIFF_HEAD_DIM), _tab_spec(st, DIFF_HEAD_DIM)],
        out_specs=out_spec,
        out_shape=out_shape,
        compiler_params=_cp("parallel", "parallel"),
        name="diff_proj_" + mode,
    )(n, w_qkv, c_tab, s_tab)


def _diff_attn_kernel(st, tq, tk, lam_init, qt_ref, k_ref, vt_ref, lq1_ref, lk1_ref, lq2_ref, lk2_ref, sg_ref, o_ref,
                      sa0, sb0, sa1, sb1, m1, l1, a1, m2, l2, a2):
    hd = DIFF_HEAD_DIM
    for m_ref, l_ref, a_ref in ((m1, l1, a1), (m2, l2, a2)):
        m_ref[...] = jnp.full_like(m_ref, NEG_BIG)
        l_ref[...] = jnp.zeros_like(l_ref)
        a_ref[...] = jnp.zeros_like(a_ref)

    def qk(j, bufs):
        kc = k_ref[0, _chunk(j, tk), :]
        bufs[0][...] = jnp.dot(kc[:, :hd], qt_ref[0, 0, :hd, :], preferred_element_type=F32)
        bufs[1][...] = jnp.dot(kc[:, hd:], qt_ref[0, 0, hd:, :], preferred_element_type=F32)

    def proc(j, bufs, masked):
        vt = vt_ref[0, 0, :, _chunk(j, tk)]
        _softmax_chunk_t(st, tk, bufs[0], j, vt, m1, l1, a1, masked)
        _softmax_chunk_t(st, tk, bufs[1], j, vt, m2, l2, a2, masked)

    _pipelined_key_loop(st, tq, tk, pl.program_id(2), qk, proc, (sa0, sb0), (sa1, sb1))
    lam = (jnp.exp(jnp.sum(lq1_ref[...] * lk1_ref[...], axis=-1, keepdims=True))
           - jnp.exp(jnp.sum(lq2_ref[...] * lk2_ref[...], axis=-1, keepdims=True)) + lam_init)
    o = a1[...] / l1[...] - lam * (a2[...] / l2[...])
    o = o * lax.rsqrt(jnp.mean(o * o, axis=0, keepdims=True) + EPS) * sg_ref[...]
    o_ref[0] = (o * (1.0 - lam_init)).T.astype(o_ref.dtype)


def _diff_attn_call(st, qt, k, vt, lq1, lk1, lq2, lk2, subln_g, heads, lam_init):
    tq, tk = _attn_tiles(st, 512)
    hw = 2 * DIFF_HEAD_DIM
    vec = lambda a: a.reshape(1, -1)
    return pl.pallas_call(
        functools.partial(_diff_attn_kernel, st, tq, tk, lam_init),
        grid=(st.b, heads, st.s // tq + 1),
        in_specs=[pl.BlockSpec((1, 1, hw, tq), lambda b, h, i: (b, h, 0, i)),
                  pl.BlockSpec((1, st.l, hw), lambda b, h, i: (b, 0, h)),
                  pl.BlockSpec((1, 1, hw, st.l), lambda b, h, i: (b, h, 0, 0)),
                  _vec_spec(DIFF_HEAD_DIM), _vec_spec(DIFF_HEAD_DIM), _vec_spec(DIFF_HEAD_DIM),
                  _vec_spec(DIFF_HEAD_DIM), pl.BlockSpec((hw, 1), lambda *_: (0, 0))],
        out_specs=pl.BlockSpec((1, tq, hw), lambda b, h, i: (b, i, h)),
        out_shape=jax.ShapeDtypeStruct((st.b, st.l, heads * hw), BF16),
        scratch_shapes=[pltpu.VMEM((tk, tq), F32)] * 4
        + [pltpu.VMEM((1, tq), F32), pltpu.VMEM((1, tq), F32), pltpu.VMEM((hw, tq), F32)] * 2,
        compiler_params=_cp("parallel", "parallel", "arbitrary"),
        name="diff_attention",
    )(qt, k, vt, vec(lq1), vec(lk1), vec(lq2), vec(lk2), subln_g.reshape(-1, 1))


def _expert_kernel(tm, n_steps, idx_hbm, x_ref, wg_ref, wu_ref, wd_ref, g_ref, gate_ref, acc_in, acc_hbm,
                   idx_smem, rows_ref, sem_idx, sem_gather, sem_scatter):
    del acc_in
    first_of_expert = pl.program_id(1) == 0
    k = pl.program_id(0) * pl.num_programs(1) + pl.program_id(1)
    slot = lax.rem(k, 2)

    def idx_copy(step, s):
        return pltpu.make_async_copy(idx_hbm.at[step], idx_smem.at[s], sem_idx.at[s])

    def row_copy(s, r, sem, to_hbm):
        hbm, vmem = acc_hbm.at[pl.ds(idx_smem[s, r], 1), :], rows_ref.at[s, pl.ds(r, 1), :]
        return pltpu.make_async_copy(vmem, hbm, sem) if to_hbm else pltpu.make_async_copy(hbm, vmem, sem)

    def for_this_slot(body):
        for s in (0, 1):
            pl.when(slot == s)(functools.partial(body, s))

    def all_rows(sem):
        return pltpu.make_async_copy(acc_hbm.at[pl.ds(0, tm), :], rows_ref.at[0], sem)

    @pl.when(k == 0)
    def _():
        idx_copy(0, 0).start()

    idx_copy(k, slot).wait()

    @pl.when(k + 1 < n_steps)
    def _():
        idx_copy(k + 1, 1 - slot).start()

    @pl.when(jnp.logical_and(k > 0, first_of_expert))
    def _():
        all_rows(sem_scatter).wait()

    def gather(s):
        for r in range(tm):
            row_copy(s, r, sem_gather, False).start()

    for_this_slot(gather)

    x = x_ref[0]
    a = jnp.dot(x, wg_ref[0], preferred_element_type=F32)
    u = jnp.dot(x, wu_ref[0], preferred_element_type=F32)
    h = (_silu(a) * u).astype(BF16)
    y = jnp.dot(h, wd_ref[0], preferred_element_type=F32) * g_ref[0]

    all_rows(sem_gather).wait()

    @pl.when(jnp.logical_not(first_of_expert))
    def _():
        all_rows(sem_scatter).wait()

    def accumulate_and_write_back(s):
        rows_ref[s] = rows_ref[s] + gate_ref[0] * y
        for r in range(tm):
            row_copy(s, r, sem_scatter, True).start()

    for_this_slot(accumulate_and_write_back)

    @pl.when(k == n_steps - 1)
    def _():
        all_rows(sem_scatter).wait()


def _expert_call(xs, wg, wu, wd, gates, idx, acc, gate, tm, gate_row0, e0):
    n_e, rows, d = xs.shape
    ff = wg.shape[-1]
    nt = rows // tm
    tiles_per_sample = nt // (gate.shape[0] // 2)
    params = pltpu.CompilerParams(dimension_semantics=("arbitrary", "arbitrary"), vmem_limit_bytes=VMEM_LIMIT_BYTES,
                                  disable_bounds_checks=True)
    return pl.pallas_call(
        functools.partial(_expert_kernel, tm, n_e * nt),
        grid=(n_e, nt),
        in_specs=[pl.BlockSpec(memory_space=pl.ANY),
                  pl.BlockSpec((1, tm, d), lambda e, i: (e, i, 0)),
                  pl.BlockSpec((1, d, ff), lambda e, i: (e + e0, 0, 0)),
                  pl.BlockSpec((1, d, ff), lambda e, i: (e + e0, 0, 0)),
                  pl.BlockSpec((1, ff, d), lambda e, i: (e + e0, 0, 0)),
                  pl.BlockSpec((1, tm, 1), lambda e, i: (e, i, 0)),
                  pl.BlockSpec((1, 1, d), lambda e, i: ((i // tiles_per_sample) * 2 + gate_row0, 0, 0)),
                  pl.BlockSpec(memory_space=pl.ANY)],
        out_specs=pl.BlockSpec(memory_space=pl.ANY),
        out_shape=jax.ShapeDtypeStruct(acc.shape, F32),
        input_output_aliases={7: 0},
        scratch_shapes=[pltpu.SMEM((2, tm), jnp.int32), pltpu.VMEM((2, tm, d), F32),
                        pltpu.SemaphoreType.DMA((2,)), pltpu.SemaphoreType.DMA, pltpu.SemaphoreType.DMA],
        compiler_params=params,
        name="moe_experts",
    )(idx.reshape(n_e * nt, tm), xs, wg, wu, wd, gates, gate, acc)


def _routed_experts(st, m, aff_seg, offset, cap, n_groups, acc, wg, wu, wd, gate, gate_row0):
    n_e = aff_seg.shape[1]
    g_sel, idx = lax.top_k(aff_seg, cap)
    idx = idx + offset
    bidx = jnp.arange(st.b)[:, None, None]
    rows = (idx + bidx * st.l).astype(jnp.int32)
    per = n_e // n_groups
    tm = _divisor(cap, 256, 16)
    for e0 in range(0, n_e, per):
        sl = slice(e0, e0 + per)
        xs = jnp.swapaxes(m[bidx, idx[:, sl]], 0, 1).reshape(per, st.b * cap, st.d)
        gates = jnp.swapaxes(g_sel[:, sl], 0, 1).reshape(per, st.b * cap, 1)
        rows_g = jnp.swapaxes(rows[:, sl], 0, 1).reshape(per, st.b * cap)
        acc = _expert_call(xs, wg, wu, wd, gates, rows_g, acc, gate, tm, gate_row0, e0)
    return acc


def _moe(st, x, g, sc, sh, gate, router_w, wg, wu, wd, with_ctx):
    n_e = router_w.shape[-1]
    m, aff = _norm_router_call(st, x, g, sc, sh, router_w.T.astype(BF16))
    wg, wu, wd = wg.astype(BF16), wu.astype(BF16), wd.astype(BF16)
    acc = x.reshape(st.b * st.l, st.d)
    cap_l = max(1, EC_FACTOR * st.s // n_e)
    acc = _routed_experts(st, m, aff[:, :, :st.s], 0, cap_l, 2 if n_e % 2 == 0 else 1, acc, wg, wu, wd, gate, 0)
    if with_ctx:
        cap_c = max(1, EC_FACTOR * st.ctx // n_e)
        acc = _routed_experts(st, m, aff[:, :, st.s:], st.s, cap_c, 1, acc, wg, wu, wd, gate, 1)
    return acc.reshape(st.b, st.l, st.d)


def _swap_halves_cols(w, width):
    k, n = w.shape
    w = w.reshape(k, n // width, 2, width // 2)
    return w[:, :, ::-1, :].reshape(k, n)


def kernel(x, c, ctx, c_ctx, ada_w, ada_b, norm_g, final_g, conv_pw1_w, conv_pw1_b, conv_dw_w, conv_dw_b, conv_ln_g, conv_ln_b, conv_pw2_w, conv_pw2_b, mla_w_dq, mla_q_norm, mla_w_uq, mla_w_dkv, mla_kv_norm, mla_w_ukv, mla_w_o, pool_w, pool_scale, diff_w_qkv, diff_lq1, diff_lk1, diff_lq2, diff_lk2, diff_subln_g, diff_w_o, moe_router, moe_w_gate, moe_w_up, moe_w_down):
    b, s, d = x.shape
    n_ctx = ctx.shape[1]
    depth = ada_w.shape[0]
    st = _Stream(b, s, n_ctx, d, _divisor(s, 512, max(n_ctx, 128)))
    xs = jnp.concatenate([x, ctx], axis=1)

    c8 = jnp.zeros((8, d), F32).at[:b].set(c).at[b].set(c_ctx)
    mod = _mod_call(c8, ada_w, ada_b)
    order = jnp.stack([jnp.arange(b), jnp.full((b,), b)], axis=1).reshape(-1)
    mod = mod[:, order, :].reshape(depth, 2 * b, 6, d)
    mods = [[mod[i, :, k, :].reshape(2 * b, 1, d) for k in range(6)] for i in range(depth)]

    heads = mla_w_uq.shape[1] // (MLA_NOPE + MLA_ROPE)
    q_rank, kv_rank = mla_q_norm.shape[0], mla_kv_norm.shape[0]
    diff_heads = diff_w_o.shape[0] // (2 * DIFF_HEAD_DIM)
    zeros_d = jnp.zeros((d,), F32)

    for i in range(depth):
        kind = i % 4
        need_ctx_out = i < depth - 1
        sh1, sc1, gt1, sh2, sc2, gt2 = mods[i]
        if kind == 0:
            n = _norm_call(st, xs, norm_g[i, 0], sc1, sh1)
            u = _glu_call(st, n, conv_pw1_w.astype(BF16), conv_pw1_b)
            v = _dwconv_call(st, u, conv_dw_w, conv_dw_b, conv_ln_g, conv_ln_b)
            xs = _proj_res_call(st, v, conv_pw2_w.astype(BF16), conv_pw2_b, xs, gt1)
        elif kind == 1:
            n = _norm_call(st, xs, norm_g[i, 0], sc1, sh1)
            c_tab, s_tab = _rope_tables(st, MLA_ROPE)
            cs_tab = jnp.concatenate([c_tab, s_tab], axis=-1)
            w_kr = mla_w_dkv[:, kv_rank:]
            w_cat = jnp.concatenate([mla_w_dq, mla_w_dkv[:, :kv_rank], w_kr, _swap_halves_cols(w_kr, MLA_ROPE)],
                                    axis=1).astype(BF16)
            cq, ckv, kr = _mla_down_call(st, n, w_cat, mla_q_norm, mla_kv_norm, cs_tab)
            w_uq = mla_w_uq.reshape(q_rank, heads, MLA_NOPE + MLA_ROPE)
            w_uq_rope = w_uq[:, :, MLA_NOPE:].reshape(q_rank, heads * MLA_ROPE)
            w_uq_ext = jnp.concatenate(
                [w_uq, _swap_halves_cols(w_uq_rope, MLA_ROPE).reshape(q_rank, heads, MLA_ROPE)], axis=-1)
            qscale = (MLA_NOPE + MLA_ROPE) ** -0.5 * LOG2E
            qt = _mla_uq_call(st, cq, w_uq_ext.reshape(q_rank, heads * MLA_QK_PAD).astype(BF16), cs_tab, heads, qscale)
            k, vt = _mla_ukv_call(st, ckv, mla_w_ukv.astype(BF16), kr, heads)
            o = _mla_attn_call(st, qt, k, vt, heads)
            xs = _proj_res_call(st, o, mla_w_o.astype(BF16), zeros_d, xs, gt1)
        elif kind == 2:
            xs = _pool_call(st, xs, norm_g[i, 0], sc1, sh1, pool_w.astype(BF16), pool_scale, gt1)
        else:
            n = _norm_call(st, xs, norm_g[i, 0], sc1, sh1)
            c_tab, s_tab = _rope_tables(st, DIFF_HEAD_DIM)
            w_qkv = diff_w_qkv.astype(BF16)
            qt = _diff_proj_call(st, n, w_qkv, c_tab, s_tab, diff_heads, "q", DIFF_HEAD_DIM ** -0.5 * LOG2E)
            k = _diff_proj_call(st, n, w_qkv, c_tab, s_tab, diff_heads, "k")
            vt = _diff_proj_call(st, n, w_qkv, c_tab, s_tab, diff_heads, "v")
            lam_init = 0.8 - 0.6 * math.exp(-0.3 * i)
            o = _diff_attn_call(st, qt, k, vt, diff_lq1, diff_lk1, diff_lq2, diff_lk2, diff_subln_g, diff_heads,
                                lam_init)
            xs = _proj_res_call(st, o, diff_w_o.astype(BF16), zeros_d, xs, gt1)
        xs = _moe(st, xs, norm_g[i, 1], sc2, sh2, gt2, moe_router[i], moe_w_gate[i], moe_w_up[i], moe_w_down[i],
                  need_ctx_out)
    return _final_norm_call(st, xs, final_g)
```

```python
import functools
import math

import jax
import jax.numpy as jnp
from jax import lax
from jax.experimental import pallas as pl
from jax.experimental.pallas import tpu as pltpu

F32 = jnp.float32
BF16 = jnp.bfloat16

GRID_W = 64
EPS = 1e-6
ROPE_THETA = 10000.0
MLA_NOPE = 128
MLA_ROPE = 64
MLA_V = 128
MLA_QK_PAD = 256
DIFF_HEAD_DIM = 128
POOL_WINDOWS = (2, 4, 8, 16)
EC_FACTOR = 2
HALO = 16
NEG_BIG = -1e30
LOG2E = 1.4426950408889634

VMEM_LIMIT_BYTES = 56 * 1024 * 1024


def _cp(*sem):
    return pltpu.CompilerParams(dimension_semantics=sem, vmem_limit_bytes=VMEM_LIMIT_BYTES)


def _divisor(n, target, mult):
    best = None
    d = mult
    while d <= min(n, target):
        if n % d == 0:
            best = d
        d += mult
    return best if best is not None else n


def _silu(v):
    return v * jax.nn.sigmoid(v)


def _rms_mod(x, g, sc, sh):
    ms = jnp.mean(x * x, axis=-1, keepdims=True)
    return (x * lax.rsqrt(ms + EPS) * g) * (1.0 + sc) + sh


def _mod_kernel(c_ref, w_ref, b_ref, o_ref):
    s = _silu(c_ref[...]).astype(BF16)
    o_ref[0] = jnp.dot(s, w_ref[0].astype(BF16), preferred_element_type=F32) + b_ref[0]


def _mod_call(c8, ada_w, ada_b):
    depth, d, n = ada_w.shape
    tn = _divisor(n, 1024, 128)
    return pl.pallas_call(
        _mod_kernel,
        grid=(depth, n // tn),
        in_specs=[pl.BlockSpec((8, d), lambda l, j: (0, 0)),
                  pl.BlockSpec((1, d, tn), lambda l, j: (l, 0, j)),
                  pl.BlockSpec((1, 1, tn), lambda l, j: (l, 0, j))],
        out_specs=pl.BlockSpec((1, 8, tn), lambda l, j: (l, 0, j)),
        out_shape=jax.ShapeDtypeStruct((depth, 8, n), F32),
        compiler_params=_cp("arbitrary", "arbitrary"),
        name="adaln_mod",
    )(c8, ada_w, ada_b.reshape(depth, 1, n))


class _Stream:
    def __init__(self, b, s, ctx, d, tm):
        assert s % tm == 0 and ctx <= tm and ctx % HALO == 0 and tm % HALO == 0
        self.b, self.s, self.ctx, self.d, self.tm = b, s, ctx, d, tm
        self.l = s + ctx
        self.n_lat = s // tm
        self.nt = self.n_lat + 1

    def row_spec(self, width, col=None):
        if col is None:
            return pl.BlockSpec((1, self.tm, width), lambda b, i, *_: (b, i, 0))
        return pl.BlockSpec((1, self.tm, width), lambda b, i, *_: (b, i, col))

    def mod_spec(self):
        n_lat = self.n_lat
        return pl.BlockSpec((1, 1, self.d), lambda b, i, *_: (b * 2 + i // n_lat, 0, 0))


def _vec_spec(width):
    return pl.BlockSpec((1, width), lambda *_: (0, 0))


def _norm_kernel(x_ref, g_ref, sc_ref, sh_ref, o_ref):
    o_ref[0] = _rms_mod(x_ref[0], g_ref[...], sc_ref[0], sh_ref[0]).astype(o_ref.dtype)


def _norm_call(st, x, g, sc, sh):
    return pl.pallas_call(
        _norm_kernel,
        grid=(st.b, st.nt),
        in_specs=[st.row_spec(st.d), _vec_spec(st.d), st.mod_spec(), st.mod_spec()],
        out_specs=st.row_spec(st.d),
        out_shape=jax.ShapeDtypeStruct((st.b, st.l, st.d), BF16),
        compiler_params=_cp("parallel", "parallel"),
        name="norm_mod",
    )(x, g.reshape(1, -1), sc, sh)


def _norm_router_kernel(x_ref, g_ref, sc_ref, sh_ref, rw_ref, m_ref, aff_ref):
    m = _rms_mod(x_ref[0], g_ref[...], sc_ref[0], sh_ref[0]).astype(BF16)
    m_ref[0] = m
    logits = lax.dot_general(rw_ref[...], m, (((1,), (1,)), ((), ())), preferred_element_type=F32)
    z = logits - jnp.max(logits, axis=0, keepdims=True)
    e = jnp.exp(z)
    aff_ref[0] = e / jnp.sum(e, axis=0, keepdims=True)


def _norm_router_call(st, x, g, sc, sh, router_t):
    n_e = router_t.shape[0]
    return pl.pallas_call(
        _norm_router_kernel,
        grid=(st.b, st.nt),
        in_specs=[st.row_spec(st.d), _vec_spec(st.d), st.mod_spec(), st.mod_spec(),
                  pl.BlockSpec((n_e, st.d), lambda b, i: (0, 0))],
        out_specs=[st.row_spec(st.d), pl.BlockSpec((1, n_e, st.tm), lambda b, i: (b, 0, i))],
        out_shape=[jax.ShapeDtypeStruct((st.b, st.l, st.d), BF16),
                   jax.ShapeDtypeStruct((st.b, n_e, st.l), F32)],
        compiler_params=_cp("parallel", "parallel"),
        name="moe_norm_router",
    )(x, g.reshape(1, -1), sc, sh, router_t)


def _final_norm_kernel(x_ref, g_ref, o_ref):
    x = x_ref[0]
    ms = jnp.mean(x * x, axis=-1, keepdims=True)
    o_ref[0] = x * lax.rsqrt(ms + EPS) * g_ref[...]


def _final_norm_call(st, x, g):
    return pl.pallas_call(
        _final_norm_kernel,
        grid=(st.b, st.n_lat),
        in_specs=[st.row_spec(st.d), _vec_spec(st.d)],
        out_specs=st.row_spec(st.d),
        out_shape=jax.ShapeDtypeStruct((st.b, st.s, st.d), F32),
        compiler_params=_cp("parallel", "parallel"),
        name="final_norm",
    )(x, g.reshape(1, -1))


def _proj_res_kernel(a_ref, w_ref, bias_ref, x_ref, gate_ref, o_ref):
    y = jnp.dot(a_ref[0], w_ref[...], preferred_element_type=F32) + bias_ref[...]
    o_ref[0] = x_ref[0] + gate_ref[0] * y


def _proj_res_call(st, a, w, bias, x, gate):
    k, n = w.shape
    return pl.pallas_call(
        _proj_res_kernel,
        grid=(st.b, st.nt),
        in_specs=[st.row_spec(k), pl.BlockSpec((k, n), lambda b, i: (0, 0)), _vec_spec(n),
                  st.row_spec(n), st.mod_spec()],
        out_specs=st.row_spec(n),
        out_shape=jax.ShapeDtypeStruct((st.b, st.l, n), F32),
        compiler_params=_cp("parallel", "parallel"),
        name="proj_residual",
    )(a, w, bias.reshape(1, -1), x, gate)


def _glu_kernel(a_ref, wa_ref, wb_ref, ba_ref, bb_ref, o_ref):
    a = a_ref[0]
    u = jnp.dot(a, wa_ref[...], preferred_element_type=F32) + ba_ref[...]
    v = jnp.dot(a, wb_ref[...], preferred_element_type=F32) + bb_ref[...]
    o_ref[0] = u * jax.nn.sigmoid(v)


def _glu_call(st, a, w, bias):
    k, n2 = w.shape
    n = n2 // 2
    tn = _divisor(n, 512, 128)
    nj = n // tn
    b2 = bias.reshape(1, -1)
    return pl.pallas_call(
        _glu_kernel,
        grid=(st.b, st.nt, nj),
        in_specs=[st.row_spec(k),
                  pl.BlockSpec((k, tn), lambda b, i, j: (0, j)),
                  pl.BlockSpec((k, tn), lambda b, i, j: (0, j + nj)),
                  pl.BlockSpec((1, tn), lambda b, i, j: (0, j)),
                  pl.BlockSpec((1, tn), lambda b, i, j: (0, j + nj))],
        out_specs=pl.BlockSpec((1, st.tm, tn), lambda b, i, j: (b, i, j)),
        out_shape=jax.ShapeDtypeStruct((st.b, st.l, n), F32),
        compiler_params=_cp("parallel", "parallel", "arbitrary"),
        name="conv_pw1_glu",
    )(a, w, w, b2, b2)


def _halo_specs(st):
    r = st.tm // HALO
    last = st.l // HALO - 1
    prev = pl.BlockSpec((1, HALO, st.d), lambda b, i: (b, jnp.maximum(i * r - 1, 0), 0))
    nxt = pl.BlockSpec((1, HALO, st.d), lambda b, i: (b, jnp.minimum((i + 1) * r, last), 0))
    return prev, nxt


def _segment_masks(st, i):
    is_ctx = i == st.n_lat
    seg_len = jnp.where(is_ctx, st.ctx, st.s)
    base = jnp.where(is_ctx, 0, i * st.tm)
    rows = lax.broadcasted_iota(jnp.int32, (st.tm, 1), 0)
    pos = base + rows
    valid = pos < seg_len
    prev_ok = jnp.logical_and(i != 0, jnp.logical_not(is_ctx))
    next_ok = i < st.n_lat - 1
    return valid, prev_ok, next_ok, pos, seg_len


def _dwconv_kernel(st, width, rc, cc, u_ref, up_ref, un_ref, w_ref, b_ref, g_ref, beta_ref, o_ref,
                   win_ref, sh_ref, acc_ref):
    i = pl.program_id(1)
    valid, prev_ok, next_ok, _, _ = _segment_masks(st, i)
    tm, d = st.tm, st.d
    pad = width // 2
    win_ref[0:HALO, :] = jnp.where(prev_ok, up_ref[0], 0.0)
    win_ref[HALO:HALO + tm, :] = jnp.where(valid, u_ref[0], 0.0)
    win_ref[HALO + tm:, :] = jnp.where(next_ok, un_ref[0], 0.0)
    n_sh = sh_ref.shape[1]

    for c0 in range(0, d, cc):
        for r in range(1, 8):
            sh_ref[r - 1] = win_ref[r:r + n_sh, c0:c0 + cc]

        def row_chunk(rb, carry, c0=c0):
            r0 = pl.multiple_of(rb * rc, rc)
            acc = jnp.zeros((rc, cc), F32)
            for k in range(width):
                off = HALO - pad + k
                a8, res = 8 * (off // 8), off % 8
                if res == 0:
                    tap = win_ref[pl.ds(r0 + a8, rc), c0:c0 + cc]
                else:
                    tap = sh_ref[res - 1, pl.ds(r0 + a8, rc), :]
                acc = acc + tap * w_ref[k:k + 1, c0:c0 + cc]
            acc_ref[pl.ds(r0, rc), c0:c0 + cc] = acc + b_ref[:, c0:c0 + cc]
            return carry

        lax.fori_loop(0, tm // rc, row_chunk, 0)
    y = acc_ref[...]
    mu = jnp.mean(y, axis=-1, keepdims=True)
    yc = y - mu
    var = jnp.mean(yc * yc, axis=-1, keepdims=True)
    z = yc * lax.rsqrt(var + EPS) * g_ref[...] + beta_ref[...]
    o_ref[0] = _silu(z).astype(o_ref.dtype)


def _dwconv_call(st, u, dw_w, dw_b, ln_g, ln_b):
    width = dw_w.shape[0]
    assert width // 2 < HALO
    prev, nxt = _halo_specs(st)
    rc = 32
    cc = _divisor(st.d, 256, 128)
    n_sh = st.tm + 2 * HALO - 8
    kern = functools.partial(_dwconv_kernel, st, width, rc, cc)
    return pl.pallas_call(
        kern,
        grid=(st.b, st.nt),
        in_specs=[st.row_spec(st.d), prev, nxt,
                  pl.BlockSpec((width, st.d), lambda b, i: (0, 0)),
                  _vec_spec(st.d), _vec_spec(st.d), _vec_spec(st.d)],
        out_specs=st.row_spec(st.d),
        out_shape=jax.ShapeDtypeStruct((st.b, st.l, st.d), BF16),
        scratch_shapes=[pltpu.VMEM((st.tm + 2 * HALO, st.d), F32), pltpu.VMEM((7, n_sh, cc), F32),
                        pltpu.VMEM((st.tm, st.d), F32)],
        compiler_params=_cp("parallel", "parallel"),
        name="dwconv_ln_silu",
    )(u, u, u, dw_w, dw_b.reshape(1, -1), ln_g.reshape(1, -1), ln_b.reshape(1, -1))


def _pool_kernel(st, x_ref, xp_ref, xn_ref, g_ref, sc_ref, sh_ref, w_ref, ps_ref, gate_ref, o_ref, win_ref):
    i = pl.program_id(1)
    valid, prev_ok, next_ok, pos, seg_len = _segment_masks(st, i)
    tm, d = st.tm, st.d
    g, sc, sh = g_ref[...], sc_ref[0], sh_ref[0]
    x = x_ref[0]
    n = jnp.where(valid, _rms_mod(x, g, sc, sh), 0.0)
    win_ref[0:HALO, :] = jnp.where(prev_ok, _rms_mod(xp_ref[0], g, sc, sh), 0.0)
    win_ref[HALO:HALO + tm, :] = n
    win_ref[HALO + tm:, :] = jnp.where(next_ok, _rms_mod(xn_ref[0], g, sc, sh), 0.0)
    grp = d // len(POOL_WINDOWS)
    for gi, w in enumerate(POOL_WINDOWS):
        c0 = gi * grp
        tot = jnp.zeros((tm, grp), F32)
        for j in range(-(w // 2), w - w // 2):
            tot = tot + win_ref[HALO + j:HALO + j + tm, c0:c0 + grp]
        cnt = jnp.minimum(pos - w // 2 + w, seg_len) - jnp.maximum(pos - w // 2, 0)
        cnt = jnp.maximum(cnt, 1).astype(F32)
        diff = (tot / cnt - n[:, c0:c0 + grp]).astype(BF16)
        y = jnp.dot(diff, w_ref[gi], preferred_element_type=F32) * ps_ref[:, c0:c0 + grp]
        o_ref[0, :, c0:c0 + grp] = x[:, c0:c0 + grp] + gate_ref[0][:, c0:c0 + grp] * y


def _pool_call(st, x, g, sc, sh, pool_w, pool_scale, gate):
    prev, nxt = _halo_specs(st)
    ng, grp, _ = pool_w.shape
    return pl.pallas_call(
        functools.partial(_pool_kernel, st),
        grid=(st.b, st.nt),
        in_specs=[st.row_spec(st.d), prev, nxt, _vec_spec(st.d), st.mod_spec(), st.mod_spec(),
                  pl.BlockSpec((ng, grp, grp), lambda b, i: (0, 0, 0)), _vec_spec(st.d), st.mod_spec()],
        out_specs=st.row_spec(st.d),
        out_shape=jax.ShapeDtypeStruct((st.b, st.l, st.d), F32),
        scratch_shapes=[pltpu.VMEM((st.tm + 2 * HALO, st.d), F32)],
        compiler_params=_cp("parallel", "parallel"),
        name="pool_mixer",
    )(x, x, x, g.reshape(1, -1), sc, sh, pool_w, pool_scale.reshape(1, -1), gate)


def _rope_tables(st, rot_dim):
    quarter = rot_dim // 4
    rows = st.s // GRID_W
    inv = ROPE_THETA ** (-jnp.arange(quarter, dtype=F32) / quarter)
    row_ang = jnp.arange(rows, dtype=F32)[:, None, None] * inv
    col_ang = jnp.arange(GRID_W, dtype=F32)[None, :, None] * inv
    ang = jnp.concatenate([jnp.broadcast_to(row_ang, (rows, GRID_W, quarter)),
                           jnp.broadcast_to(col_ang, (rows, GRID_W, quarter))], axis=-1).reshape(st.s, 2 * quarter)
    cos, sin = jnp.cos(ang), jnp.sin(ang)
    c_tab = jnp.concatenate([cos, cos], axis=-1)
    s_tab = jnp.concatenate([-sin, sin], axis=-1)
    c_tab = jnp.concatenate([c_tab, jnp.ones((st.ctx, rot_dim), F32)], axis=0)
    s_tab = jnp.concatenate([s_tab, jnp.zeros((st.ctx, rot_dim), F32)], axis=0)
    return c_tab, s_tab


def _tab_spec(st, width):
    return pl.BlockSpec((st.tm, width), lambda b, i, *_: (i, 0))


def _folded_rope(r, cs):
    t = r * cs
    t = t + pltpu.roll(t, 64, axis=1)
    lane = lax.broadcasted_iota(jnp.int32, t.shape, 1)
    return jnp.where(lane < MLA_ROPE, t, 0.0)


def _mla_down_kernel(q_rank, kv_rank, n_ref, w_ref, qn_ref, kvn_ref, cs_ref, cq_ref, ckv_ref, kr_ref):
    acc = jnp.dot(n_ref[0], w_ref[...], preferred_element_type=F32)
    cq = acc[:, :q_rank]
    cq_ref[0] = (cq * lax.rsqrt(jnp.mean(cq * cq, axis=-1, keepdims=True) + EPS) * qn_ref[...]).astype(BF16)
    ckv = acc[:, q_rank:q_rank + kv_rank]
    ckv_ref[0] = (ckv * lax.rsqrt(jnp.mean(ckv * ckv, axis=-1, keepdims=True) + EPS) * kvn_ref[...]).astype(BF16)
    kr_ref[0] = _folded_rope(acc[:, q_rank + kv_rank:], cs_ref[...]).astype(BF16)


def _mla_down_call(st, n, w_cat, q_norm, kv_norm, cs_tab):
    k, ncols = w_cat.shape
    q_rank, kv_rank = q_norm.shape[0], kv_norm.shape[0]
    assert ncols == q_rank + kv_rank + 128 and q_rank % 128 == 0 and kv_rank % 128 == 0
    return pl.pallas_call(
        functools.partial(_mla_down_kernel, q_rank, kv_rank),
        grid=(st.b, st.nt),
        in_specs=[st.row_spec(k), pl.BlockSpec((k, ncols), lambda b, i: (0, 0)),
                  _vec_spec(q_rank), _vec_spec(kv_rank), _tab_spec(st, 128)],
        out_specs=[st.row_spec(q_rank), st.row_spec(kv_rank), st.row_spec(128)],
        out_shape=[jax.ShapeDtypeStruct((st.b, st.l, q_rank), BF16),
                   jax.ShapeDtypeStruct((st.b, st.l, kv_rank), BF16),
                   jax.ShapeDtypeStruct((st.b, st.l, 128), BF16)],
        compiler_params=_cp("parallel", "parallel"),
        name="mla_down",
    )(n, w_cat, q_norm.reshape(1, -1), kv_norm.reshape(1, -1), cs_tab)


def _mla_uq_kernel(heads, qscale, cq_ref, w_ref, cs_ref, qt_ref):
    cq, cs = cq_ref[0], cs_ref[...]
    for h in range(heads):
        acc = jnp.dot(cq, w_ref[:, h * MLA_QK_PAD:(h + 1) * MLA_QK_PAD], preferred_element_type=F32)
        q = jnp.concatenate([acc[:, :MLA_NOPE], _folded_rope(acc[:, MLA_NOPE:], cs)], axis=1) * qscale
        qt_ref[0, h] = q.T.astype(BF16)


def _mla_uq_call(st, cq, w_ext, cs_tab, heads, qscale):
    k = cq.shape[-1]
    return pl.pallas_call(
        functools.partial(_mla_uq_kernel, heads, qscale),
        grid=(st.b, st.nt),
        in_specs=[st.row_spec(k), pl.BlockSpec((k, heads * MLA_QK_PAD), lambda b, i: (0, 0)), _tab_spec(st, 128)],
        out_specs=pl.BlockSpec((1, heads, MLA_QK_PAD, st.tm), lambda b, i: (b, 0, 0, i)),
        out_shape=jax.ShapeDtypeStruct((st.b, heads, MLA_QK_PAD, st.l), BF16),
        compiler_params=_cp("parallel", "parallel"),
        name="mla_up_q",
    )(cq, w_ext, cs_tab)


def _mla_ukv_kernel(heads, ckv_ref, w_ref, kr_ref, k_ref, vt_ref):
    ckv, kr = ckv_ref[0], kr_ref[0]
    hw = MLA_NOPE + MLA_V
    for h in range(heads):
        acc = jnp.dot(ckv, w_ref[:, h * hw:(h + 1) * hw], preferred_element_type=F32)
        k_ref[0, h, :, :MLA_NOPE] = acc[:, :MLA_NOPE].astype(BF16)
        k_ref[0, h, :, MLA_NOPE:] = kr
        vt_ref[0, h] = acc[:, MLA_NOPE:].T.astype(BF16)


def _mla_ukv_call(st, ckv, w_ukv, kr, heads):
    k = ckv.shape[-1]
    hw = MLA_NOPE + MLA_V
    return pl.pallas_call(
        functools.partial(_mla_ukv_kernel, heads),
        grid=(st.b, st.nt),
        in_specs=[st.row_spec(k), pl.BlockSpec((k, heads * hw), lambda b, i: (0, 0)), st.row_spec(128)],
        out_specs=[pl.BlockSpec((1, heads, st.tm, MLA_QK_PAD), lambda b, i: (b, 0, i, 0)),
                   pl.BlockSpec((1, heads, MLA_V, st.tm), lambda b, i: (b, 0, 0, i))],
        out_shape=[jax.ShapeDtypeStruct((st.b, heads, st.l, MLA_QK_PAD), BF16),
                   jax.ShapeDtypeStruct((st.b, heads, MLA_V, st.l), BF16)],
        compiler_params=_cp("parallel", "parallel"),
        name="mla_up_kv",
    )(ckv, w_ukv, kr)


def _softmax_chunk_t(st, tk, s_ref, j, vt, m_ref, l_ref, acc_ref, masked):
    s = s_ref[...]
    if masked:
        row = j * tk + lax.broadcasted_iota(jnp.int32, s.shape, 0)
        s = jnp.where(row >= st.s, s, NEG_BIG)
    m_prev = m_ref[...]
    m_new = jnp.maximum(m_prev, jnp.max(s, axis=0, keepdims=True))
    alpha = jnp.exp2(m_prev - m_new)
    p = jnp.exp2(s - m_new)
    l_ref[...] = alpha * l_ref[...] + jnp.sum(p, axis=0, keepdims=True)
    acc_ref[...] = alpha * acc_ref[...] + jnp.dot(vt, p.astype(BF16), preferred_element_type=F32)
    m_ref[...] = m_new


def _pipelined_key_loop(st, tq, tk, qi, qk, qk_next, proc, bufs0, bufs1, bufs2):
    n_lat_q = st.s // tq
    nch = st.l // tk
    carry_first_chunk = nch % 2 == 1 and nch >= 3

    def pair(jj, carry):
        qk(2 * jj + 1, bufs1)
        proc(2 * jj, bufs0, False)
        qk(2 * jj + 2, bufs0)
        proc(2 * jj + 1, bufs1, False)
        return carry

    if carry_first_chunk:
        @pl.when(qi == 0)
        def _():
            qk(0, bufs2)

        @pl.when(qi < n_lat_q)
        def _():
            qk(1, bufs1)
            proc(0, bufs2, False)
            qk(2, bufs0)
            proc(1, bufs1, False)
            lax.fori_loop(1, (nch - 1) // 2, pair, 0)
            qk_next(bufs2)
            proc(nch - 1, bufs0, False)
    else:
        @pl.when(qi < n_lat_q)
        def _():
            qk(0, bufs0)
            lax.fori_loop(0, (nch - 1) // 2, pair, 0)
            if nch % 2 == 1:
                proc(nch - 1, bufs0, False)
            else:
                qk(nch - 1, bufs1)
                proc(nch - 2, bufs0, False)
                proc(nch - 1, bufs1, False)

    @pl.when(qi == n_lat_q)
    def _():
        for j in range(st.s // tk, nch):
            qk(j, bufs0)
            proc(j, bufs0, True)


def _chunk(j, tk):
    return pl.ds(pl.multiple_of(j * tk, tk), tk)


def _mla_attn_kernel(st, tq, tk, qt_ref, qt_next_ref, k_ref, vt_ref, o_ref, s0_ref, s1_ref, s2_ref,
                     m_ref, l_ref, acc_ref):
    m_ref[...] = jnp.full_like(m_ref, NEG_BIG)
    l_ref[...] = jnp.zeros_like(l_ref)
    acc_ref[...] = jnp.zeros_like(acc_ref)

    def scores(j, q_ref, s_ref):
        s_ref[...] = jnp.dot(k_ref[0, 0, _chunk(j, tk), :], q_ref[0, 0], preferred_element_type=F32)

    def proc(j, s_ref, masked):
        _softmax_chunk_t(st, tk, s_ref, j, vt_ref[0, 0, :, _chunk(j, tk)], m_ref, l_ref, acc_ref, masked)

    _pipelined_key_loop(st, tq, tk, pl.program_id(2),
                        lambda j, s_ref: scores(j, qt_ref, s_ref), lambda s_ref: scores(0, qt_next_ref, s_ref),
                        proc, s0_ref, s1_ref, s2_ref)
    o_ref[0] = (acc_ref[...] / l_ref[...]).T.astype(o_ref.dtype)


def _attn_tiles(st, tq_target):
    tq = _divisor(st.s, tq_target, 128)
    assert st.ctx <= tq
    tk = _divisor(st.l, 1536, 256 if st.l % 256 == 0 else 128)
    return tq, tk


def _mla_attn_call(st, qt, k, vt, heads):
    tq, tk = _attn_tiles(st, 1024)
    last = st.s // tq
    return pl.pallas_call(
        functools.partial(_mla_attn_kernel, st, tq, tk),
        grid=(st.b, heads, last + 1),
        in_specs=[pl.BlockSpec((1, 1, MLA_QK_PAD, tq), lambda b, h, i: (b, h, 0, i)),
                  pl.BlockSpec((1, 1, MLA_QK_PAD, tq), lambda b, h, i: (b, h, 0, jnp.minimum(i + 1, last))),
                  pl.BlockSpec((1, 1, st.l, MLA_QK_PAD), lambda b, h, i: (b, h, 0, 0)),
                  pl.BlockSpec((1, 1, MLA_V, st.l), lambda b, h, i: (b, h, 0, 0))],
        out_specs=pl.BlockSpec((1, tq, MLA_V), lambda b, h, i: (b, i, h)),
        out_shape=jax.ShapeDtypeStruct((st.b, st.l, heads * MLA_V), BF16),
        scratch_shapes=[pltpu.VMEM((tk, tq), F32)] * 3
        + [pltpu.VMEM((1, tq), F32), pltpu.VMEM((1, tq), F32), pltpu.VMEM((MLA_V, tq), F32)],
        compiler_params=_cp("arbitrary", "arbitrary", "arbitrary"),
        name="mla_attention",
    )(qt, qt, k, vt)


def _diff_proj_kernel(heads, mode, qscale, n_ref, w_ref, c_ref, s_ref, o_ref):
    hd = DIFF_HEAD_DIM
    acc = jnp.dot(n_ref[0], w_ref[...], preferred_element_type=F32)
    if mode != "v":
        c, s = c_ref[...], s_ref[...]
    for h in range(heads):
        if mode == "v":
            o_ref[0, h] = acc[:, 2 * h * hd:2 * (h + 1) * hd].T.astype(BF16)
            continue
        halves = []
        for n in range(2):
            xh = acc[:, (2 * h + n) * hd:(2 * h + n + 1) * hd]
            halves.append(xh * c + pltpu.roll(xh, hd // 2, axis=1) * s)
        if mode == "k":
            o_ref[0, :, 2 * h * hd:(2 * h + 1) * hd] = halves[0].astype(BF16)
            o_ref[0, :, (2 * h + 1) * hd:2 * (h + 1) * hd] = halves[1].astype(BF16)
        else:
            o_ref[0, h] = (jnp.concatenate(halves, axis=1) * qscale).T.astype(BF16)


def _diff_proj_call(st, n, w_qkv, c_tab, s_tab, heads, mode, qscale=1.0):
    k = w_qkv.shape[0]
    hw = 2 * DIFF_HEAD_DIM
    third = {"q": 0, "k": 1, "v": 2}[mode]
    if mode == "k":
        out_spec = st.row_spec(heads * hw)
        out_shape = jax.ShapeDtypeStruct((st.b, st.l, heads * hw), BF16)
    else:
        out_spec = pl.BlockSpec((1, heads, hw, st.tm), lambda b, i: (b, 0, 0, i))
        out_shape = jax.ShapeDtypeStruct((st.b, heads, hw, st.l), BF16)
    return pl.pallas_call(
        functools.partial(_diff_proj_kernel, heads, mode, qscale),
        grid=(st.b, st.nt),
        in_specs=[st.row_spec(k), pl.BlockSpec((k, heads * hw), lambda b, i: (0, third)),
                  _tab_spec(st, DIFF_HEAD_DIM), _tab_spec(st, DIFF_HEAD_DIM)],
        out_specs=out_spec,
        out_shape=out_shape,
        compiler_params=_cp("parallel", "parallel"),
        name="diff_proj_" + mode,
    )(n, w_qkv, c_tab, s_tab)


def _diff_attn_kernel(st, tq, tk, lam_init, qt_ref, qt_next_ref, k_ref, vt_ref, lq1_ref, lk1_ref, lq2_ref, lk2_ref,
                      sg_ref, o_ref, sa0, sb0, sa1, sb1, sa2, sb2, m1, l1, a1, m2, l2, a2):
    hd = DIFF_HEAD_DIM
    for m_ref, l_ref, a_ref in ((m1, l1, a1), (m2, l2, a2)):
        m_ref[...] = jnp.full_like(m_ref, NEG_BIG)
        l_ref[...] = jnp.zeros_like(l_ref)
        a_ref[...] = jnp.zeros_like(a_ref)

    def scores(j, q_ref, bufs):
        kc = k_ref[0, _chunk(j, tk), :]
        bufs[0][...] = jnp.dot(kc[:, :hd], q_ref[0, 0, :hd, :], preferred_element_type=F32)
        bufs[1][...] = jnp.dot(kc[:, hd:], q_ref[0, 0, hd:, :], preferred_element_type=F32)

    def proc(j, bufs, masked):
        vt = vt_ref[0, 0, :, _chunk(j, tk)]
        _softmax_chunk_t(st, tk, bufs[0], j, vt, m1, l1, a1, masked)
        _softmax_chunk_t(st, tk, bufs[1], j, vt, m2, l2, a2, masked)

    _pipelined_key_loop(st, tq, tk, pl.program_id(2),
                        lambda j, bufs: scores(j, qt_ref, bufs), lambda bufs: scores(0, qt_next_ref, bufs),
                        proc, (sa0, sb0), (sa1, sb1), (sa2, sb2))
    lam = (jnp.exp(jnp.sum(lq1_ref[...] * lk1_ref[...], axis=-1, keepdims=True))
           - jnp.exp(jnp.sum(lq2_ref[...] * lk2_ref[...], axis=-1, keepdims=True)) + lam_init)
    o = a1[...] / l1[...] - lam * (a2[...] / l2[...])
    o = o * lax.rsqrt(jnp.mean(o * o, axis=0, keepdims=True) + EPS) * sg_ref[...]
    o_ref[0] = (o * (1.0 - lam_init)).T.astype(o_ref.dtype)


def _diff_attn_call(st, qt, k, vt, lq1, lk1, lq2, lk2, subln_g, heads, lam_init):
    tq, tk = _attn_tiles(st, 512)
    hw = 2 * DIFF_HEAD_DIM
    vec = lambda a: a.reshape(1, -1)
    last = st.s // tq
    return pl.pallas_call(
        functools.partial(_diff_attn_kernel, st, tq, tk, lam_init),
        grid=(st.b, heads, last + 1),
        in_specs=[pl.BlockSpec((1, 1, hw, tq), lambda b, h, i: (b, h, 0, i)),
                  pl.BlockSpec((1, 1, hw, tq), lambda b, h, i: (b, h, 0, jnp.minimum(i + 1, last))),
                  pl.BlockSpec((1, st.l, hw), lambda b, h, i: (b, 0, h), pipeline_mode=pl.Buffered(1)),
                  pl.BlockSpec((1, 1, hw, st.l), lambda b, h, i: (b, h, 0, 0), pipeline_mode=pl.Buffered(1)),
                  _vec_spec(DIFF_HEAD_DIM), _vec_spec(DIFF_HEAD_DIM), _vec_spec(DIFF_HEAD_DIM),
                  _vec_spec(DIFF_HEAD_DIM), pl.BlockSpec((hw, 1), lambda *_: (0, 0))],
        out_specs=pl.BlockSpec((1, tq, hw), lambda b, h, i: (b, i, h)),
        out_shape=jax.ShapeDtypeStruct((st.b, st.l, heads * hw), BF16),
        scratch_shapes=[pltpu.VMEM((tk, tq), F32)] * 6
        + [pltpu.VMEM((1, tq), F32), pltpu.VMEM((1, tq), F32), pltpu.VMEM((hw, tq), F32)] * 2,
        compiler_params=_cp("arbitrary", "arbitrary", "arbitrary"),
        name="diff_attention",
    )(qt, qt, k, vt, vec(lq1), vec(lk1), vec(lq2), vec(lk2), subln_g.reshape(-1, 1))


W_CHUNKS = 16


def _expert_kernel(tm, n_local, nt, layer, e0, idx_hbm, x_ref, wg_hbm, wu_hbm, wd_hbm, g_ref, gate_ref, acc_in,
                   acc_hbm, idx_smem, rows_ref, wg_bf, wu_bf, wd_bf, stage_g, stage_u, stage_d,
                   sem_idx, sem_gather, sem_scatter, sem_w):
    del acc_in
    e_local, tile = pl.program_id(0), pl.program_id(1)
    n_steps = n_local * nt
    first_of_expert = tile == 0
    k = e_local * nt + tile
    slot = lax.rem(k, 2)
    wslot = lax.rem(e_local, 2)
    rows_g, rows_d = wg_bf.shape[1] // W_CHUNKS, wd_bf.shape[1] // W_CHUNKS

    def w_copies(expert, c, s):
        cg, cd = pl.ds(pl.multiple_of(c * rows_g, rows_g), rows_g), pl.ds(pl.multiple_of(c * rows_d, rows_d), rows_d)
        return (pltpu.make_async_copy(wg_hbm.at[layer, expert, cg, :], stage_g.at[s], sem_w.at[s]),
                pltpu.make_async_copy(wu_hbm.at[layer, expert, cg, :], stage_u.at[s], sem_w.at[s]),
                pltpu.make_async_copy(wd_hbm.at[layer, expert, cd, :], stage_d.at[s], sem_w.at[s]))

    def w_start(expert, c, s):
        for cp in w_copies(expert, c, s):
            cp.start()

    def w_finish(expert, c, s, dst):
        for cp in w_copies(expert, c, s):
            cp.wait()
        cg, cd = pl.ds(pl.multiple_of(c * rows_g, rows_g), rows_g), pl.ds(pl.multiple_of(c * rows_d, rows_d), rows_d)
        wg_bf[dst, cg, :] = stage_g[s].astype(BF16)
        wu_bf[dst, cg, :] = stage_u[s].astype(BF16)
        wd_bf[dst, cd, :] = stage_d[s].astype(BF16)

    def w_stream(expert, chunks, dst, first_started):
        if not first_started:
            w_start(expert, chunks[0], 0)
        for n, c in enumerate(chunks):
            if n + 1 < len(chunks):
                w_start(expert, chunks[n + 1], (n + 1) % 2)
            w_finish(expert, c, n % 2, dst)

    @pl.when(k == 0)
    def _():
        w_stream(e0, list(range(W_CHUNKS)), 0, False)

    per_tile = W_CHUNKS // nt
    next_chunks = [tile * per_tile + n for n in range(per_tile)]
    has_next = e_local + 1 < n_local

    @pl.when(has_next)
    def _():
        w_start(e0 + e_local + 1, next_chunks[0], 0)

    def idx_copy(step, s):
        return pltpu.make_async_copy(idx_hbm.at[step], idx_smem.at[s], sem_idx.at[s])

    def row_copy(s, r, sem, to_hbm):
        hbm, vmem = acc_hbm.at[pl.ds(idx_smem[s, r], 1), :], rows_ref.at[s, pl.ds(r, 1), :]
        return pltpu.make_async_copy(vmem, hbm, sem) if to_hbm else pltpu.make_async_copy(hbm, vmem, sem)

    def for_this_slot(body):
        for s in (0, 1):
            pl.when(slot == s)(functools.partial(body, s))

    def all_rows(sem):
        return pltpu.make_async_copy(acc_hbm.at[pl.ds(0, tm), :], rows_ref.at[0], sem)

    @pl.when(k == 0)
    def _():
        idx_copy(0, 0).start()

    idx_copy(k, slot).wait()

    @pl.when(k + 1 < n_steps)
    def _():
        idx_copy(k + 1, 1 - slot).start()

    @pl.when(jnp.logical_and(k > 0, first_of_expert))
    def _():
        all_rows(sem_scatter).wait()

    def gather(s):
        for r in range(tm):
            row_copy(s, r, sem_gather, False).start()

    for_this_slot(gather)

    x = x_ref[0]
    a = jnp.dot(x, wg_bf[wslot], preferred_element_type=F32)
    u = jnp.dot(x, wu_bf[wslot], preferred_element_type=F32)
    h = (_silu(a) * u).astype(BF16)
    y = jnp.dot(h, wd_bf[wslot], preferred_element_type=F32) * g_ref[0]

    @pl.when(has_next)
    def _():
        w_stream(e0 + e_local + 1, next_chunks, 1 - wslot, True)

    all_rows(sem_gather).wait()

    @pl.when(jnp.logical_not(first_of_expert))
    def _():
        all_rows(sem_scatter).wait()

    def accumulate_and_write_back(s):
        rows_ref[s] = rows_ref[s] + gate_ref[0] * y
        for r in range(tm):
            row_copy(s, r, sem_scatter, True).start()

    for_this_slot(accumulate_and_write_back)

    @pl.when(k == n_steps - 1)
    def _():
        all_rows(sem_scatter).wait()


def _expert_call(xs, wg, wu, wd, layer, gates, idx, acc, gate, tm, gate_row0, e0):
    n_e, rows, d = xs.shape
    ff = wg.shape[-1]
    nt = rows // tm
    assert W_CHUNKS % nt == 0 and d % (16 * W_CHUNKS) == 0 and ff % (16 * W_CHUNKS) == 0
    tiles_per_sample = nt // (gate.shape[0] // 2)
    params = pltpu.CompilerParams(dimension_semantics=("arbitrary", "arbitrary"), vmem_limit_bytes=VMEM_LIMIT_BYTES,
                                  disable_bounds_checks=True)
    hbm = pl.BlockSpec(memory_space=pl.ANY)
    return pl.pallas_call(
        functools.partial(_expert_kernel, tm, n_e, nt, layer, e0),
        grid=(n_e, nt),
        in_specs=[hbm,
                  pl.BlockSpec((1, tm, d), lambda e, i: (e, i, 0)),
                  hbm, hbm, hbm,
                  pl.BlockSpec((1, tm, 1), lambda e, i: (e, i, 0)),
                  pl.BlockSpec((1, 1, d), lambda e, i: ((i // tiles_per_sample) * 2 + gate_row0, 0, 0)),
                  hbm],
        out_specs=hbm,
        out_shape=jax.ShapeDtypeStruct(acc.shape, F32),
        input_output_aliases={7: 0},
        scratch_shapes=[pltpu.SMEM((2, tm), jnp.int32), pltpu.VMEM((2, tm, d), F32),
                        pltpu.VMEM((2, d, ff), BF16), pltpu.VMEM((2, d, ff), BF16), pltpu.VMEM((2, ff, d), BF16),
                        pltpu.VMEM((2, d // W_CHUNKS, ff), F32), pltpu.VMEM((2, d // W_CHUNKS, ff), F32),
                        pltpu.VMEM((2, ff // W_CHUNKS, d), F32),
                        pltpu.SemaphoreType.DMA((2,)), pltpu.SemaphoreType.DMA, pltpu.SemaphoreType.DMA,
                        pltpu.SemaphoreType.DMA((2,))],
        compiler_params=params,
        name="moe_experts",
    )(idx.reshape(n_e * nt, tm), xs, wg, wu, wd, gates, gate, acc)


def _routed_experts(st, m, aff_seg, offset, cap, n_groups, acc, wg, wu, wd, layer, gate, gate_row0):
    n_e = aff_seg.shape[1]
    g_sel, idx = lax.top_k(aff_seg, cap)
    idx = idx + offset
    bidx = jnp.arange(st.b)[:, None, None]
    rows = (idx + bidx * st.l).astype(jnp.int32)
    per = n_e // n_groups
    tm = _divisor(cap, 256, 16)
    for e0 in range(0, n_e, per):
        sl = slice(e0, e0 + per)
        xs = jnp.swapaxes(m[bidx, idx[:, sl]], 0, 1).reshape(per, st.b * cap, st.d)
        gates = jnp.swapaxes(g_sel[:, sl], 0, 1).reshape(per, st.b * cap, 1)
        rows_g = jnp.swapaxes(rows[:, sl], 0, 1).reshape(per, st.b * cap)
        acc = _expert_call(xs, wg, wu, wd, layer, gates, rows_g, acc, gate, tm, gate_row0, e0)
    return acc


def _moe(st, x, g, sc, sh, gate, router_w, wg, wu, wd, layer, with_ctx):
    n_e = router_w.shape[-1]
    m, aff = _norm_router_call(st, x, g, sc, sh, router_w.T.astype(BF16))
    acc = x.reshape(st.b * st.l, st.d)
    cap_l = max(1, EC_FACTOR * st.s // n_e)
    acc = _routed_experts(st, m, aff[:, :, :st.s], 0, cap_l, 2 if n_e % 2 == 0 else 1, acc, wg, wu, wd, layer,
                          gate, 0)
    if with_ctx:
        cap_c = max(1, EC_FACTOR * st.ctx // n_e)
        acc = _routed_experts(st, m, aff[:, :, st.s:], st.s, cap_c, 1, acc, wg, wu, wd, layer, gate, 1)
    return acc.reshape(st.b, st.l, st.d)


def _swap_halves_cols(w, width):
    k, n = w.shape
    w = w.reshape(k, n // width, 2, width // 2)
    return w[:, :, ::-1, :].reshape(k, n)


def kernel(x, c, ctx, c_ctx, ada_w, ada_b, norm_g, final_g, conv_pw1_w, conv_pw1_b, conv_dw_w, conv_dw_b, conv_ln_g, conv_ln_b, conv_pw2_w, conv_pw2_b, mla_w_dq, mla_q_norm, mla_w_uq, mla_w_dkv, mla_kv_norm, mla_w_ukv, mla_w_o, pool_w, pool_scale, diff_w_qkv, diff_lq1, diff_lk1, diff_lq2, diff_lk2, diff_subln_g, diff_w_o, moe_router, moe_w_gate, moe_w_up, moe_w_down):
    b, s, d = x.shape
    n_ctx = ctx.shape[1]
    depth = ada_w.shape[0]
    st = _Stream(b, s, n_ctx, d, _divisor(s, 512, max(n_ctx, 128)))
    xs = jnp.concatenate([x, ctx], axis=1)

    c8 = jnp.zeros((8, d), F32).at[:b].set(c).at[b].set(c_ctx)
    mod = _mod_call(c8, ada_w, ada_b)
    order = jnp.stack([jnp.arange(b), jnp.full((b,), b)], axis=1).reshape(-1)
    mod = mod[:, order, :].reshape(depth, 2 * b, 6, d)
    mods = [[mod[i, :, k, :].reshape(2 * b, 1, d) for k in range(6)] for i in range(depth)]

    heads = mla_w_uq.shape[1] // (MLA_NOPE + MLA_ROPE)
    q_rank, kv_rank = mla_q_norm.shape[0], mla_kv_norm.shape[0]
    diff_heads = diff_w_o.shape[0] // (2 * DIFF_HEAD_DIM)
    zeros_d = jnp.zeros((d,), F32)

    for i in range(depth):
        kind = i % 4
        need_ctx_out = i < depth - 1
        sh1, sc1, gt1, sh2, sc2, gt2 = mods[i]
        if kind == 0:
            n = _norm_call(st, xs, norm_g[i, 0], sc1, sh1)
            u = _glu_call(st, n, conv_pw1_w.astype(BF16), conv_pw1_b)
            v = _dwconv_call(st, u, conv_dw_w, conv_dw_b, conv_ln_g, conv_ln_b)
            xs = _proj_res_call(st, v, conv_pw2_w.astype(BF16), conv_pw2_b, xs, gt1)
        elif kind == 1:
            n = _norm_call(st, xs, norm_g[i, 0], sc1, sh1)
            c_tab, s_tab = _rope_tables(st, MLA_ROPE)
            cs_tab = jnp.concatenate([c_tab, s_tab], axis=-1)
            w_kr = mla_w_dkv[:, kv_rank:]
            w_cat = jnp.concatenate([mla_w_dq, mla_w_dkv[:, :kv_rank], w_kr, _swap_halves_cols(w_kr, MLA_ROPE)],
                                    axis=1).astype(BF16)
            cq, ckv, kr = _mla_down_call(st, n, w_cat, mla_q_norm, mla_kv_norm, cs_tab)
            w_uq = mla_w_uq.reshape(q_rank, heads, MLA_NOPE + MLA_ROPE)
            w_uq_rope = w_uq[:, :, MLA_NOPE:].reshape(q_rank, heads * MLA_ROPE)
            w_uq_ext = jnp.concatenate(
                [w_uq, _swap_halves_cols(w_uq_rope, MLA_ROPE).reshape(q_rank, heads, MLA_ROPE)], axis=-1)
            qscale = (MLA_NOPE + MLA_ROPE) ** -0.5 * LOG2E
            qt = _mla_uq_call(st, cq, w_uq_ext.reshape(q_rank, heads * MLA_QK_PAD).astype(BF16), cs_tab, heads, qscale)
            k, vt = _mla_ukv_call(st, ckv, mla_w_ukv.astype(BF16), kr, heads)
            o = _mla_attn_call(st, qt, k, vt, heads)
            xs = _proj_res_call(st, o, mla_w_o.astype(BF16), zeros_d, xs, gt1)
        elif kind == 2:
            xs = _pool_call(st, xs, norm_g[i, 0], sc1, sh1, pool_w.astype(BF16), pool_scale, gt1)
        else:
            n = _norm_call(st, xs, norm_g[i, 0], sc1, sh1)
            c_tab, s_tab = _rope_tables(st, DIFF_HEAD_DIM)
            w_qkv = diff_w_qkv.astype(BF16)
            qt = _diff_proj_call(st, n, w_qkv, c_tab, s_tab, diff_heads, "q", DIFF_HEAD_DIM ** -0.5 * LOG2E)
            k = _diff_proj_call(st, n, w_qkv, c_tab, s_tab, diff_heads, "k")
            vt = _diff_proj_call(st, n, w_qkv, c_tab, s_tab, diff_heads, "v")
            lam_init = 0.8 - 0.6 * math.exp(-0.3 * i)
            o = _diff_attn_call(st, qt, k, vt, diff_lq1, diff_lk1, diff_lq2, diff_lk2, diff_subln_g, diff_heads,
                                lam_init)
            xs = _proj_res_call(st, o, diff_w_o.astype(BF16), zeros_d, xs, gt1)
        xs = _moe(st, xs, norm_g[i, 1], sc2, sh2, gt2, moe_router[i], moe_w_gate, moe_w_up, moe_w_down, i,
                  need_ctx_out)
    return _final_norm_call(st, xs, final_g)
```

```python
import functools
import math

import jax
import jax.numpy as jnp
from jax import lax
from jax.experimental import pallas as pl
from jax.experimental.pallas import tpu as pltpu

F32 = jnp.float32
BF16 = jnp.bfloat16

GRID_W = 64
EPS = 1e-6
ROPE_THETA = 10000.0
MLA_NOPE = 128
MLA_ROPE = 64
MLA_V = 128
MLA_QK_PAD = 256
DIFF_HEAD_DIM = 128
POOL_WINDOWS = (2, 4, 8, 16)
EC_FACTOR = 2
HALO = 16
NEG_BIG = -1e30
LOG2E = 1.4426950408889634

VMEM_LIMIT_BYTES = 56 * 1024 * 1024


def _cp(*sem):
    return pltpu.CompilerParams(dimension_semantics=sem, vmem_limit_bytes=VMEM_LIMIT_BYTES)


def _divisor(n, target, mult):
    best = None
    d = mult
    while d <= min(n, target):
        if n % d == 0:
            best = d
        d += mult
    return best if best is not None else n


def _silu(v):
    return v * jax.nn.sigmoid(v)


def _rms_mod(x, g, sc, sh):
    ms = jnp.mean(x * x, axis=-1, keepdims=True)
    return (x * lax.rsqrt(ms + EPS) * g) * (1.0 + sc) + sh


def _mod_kernel(c_ref, w_ref, b_ref, o_ref):
    s = _silu(c_ref[...]).astype(BF16)
    o_ref[0] = jnp.dot(s, w_ref[0].astype(BF16), preferred_element_type=F32) + b_ref[0]


def _mod_call(c8, ada_w, ada_b):
    depth, d, n = ada_w.shape
    tn = _divisor(n, 1024, 128)
    return pl.pallas_call(
        _mod_kernel,
        grid=(depth, n // tn),
        in_specs=[pl.BlockSpec((8, d), lambda l, j: (0, 0)),
                  pl.BlockSpec((1, d, tn), lambda l, j: (l, 0, j)),
                  pl.BlockSpec((1, 1, tn), lambda l, j: (l, 0, j))],
        out_specs=pl.BlockSpec((1, 8, tn), lambda l, j: (l, 0, j)),
        out_shape=jax.ShapeDtypeStruct((depth, 8, n), F32),
        compiler_params=_cp("arbitrary", "arbitrary"),
        name="adaln_mod",
    )(c8, ada_w, ada_b.reshape(depth, 1, n))


class _Stream:
    def __init__(self, b, s, ctx, d, tm):
        assert s % tm == 0 and ctx <= tm and ctx % HALO == 0 and tm % HALO == 0
        self.b, self.s, self.ctx, self.d, self.tm = b, s, ctx, d, tm
        self.l = s + ctx
        self.n_lat = s // tm
        self.nt = self.n_lat + 1

    def row_spec(self, width, col=None):
        if col is None:
            return pl.BlockSpec((1, self.tm, width), lambda b, i, *_: (b, i, 0))
        return pl.BlockSpec((1, self.tm, width), lambda b, i, *_: (b, i, col))

    def mod_spec(self):
        n_lat = self.n_lat
        return pl.BlockSpec((1, 1, self.d), lambda b, i, *_: (b * 2 + i // n_lat, 0, 0))


def _vec_spec(width):
    return pl.BlockSpec((1, width), lambda *_: (0, 0))


def _norm_kernel(x_ref, g_ref, sc_ref, sh_ref, o_ref):
    o_ref[0] = _rms_mod(x_ref[0], g_ref[...], sc_ref[0], sh_ref[0]).astype(o_ref.dtype)


def _norm_call(st, x, g, sc, sh):
    return pl.pallas_call(
        _norm_kernel,
        grid=(st.b, st.nt),
        in_specs=[st.row_spec(st.d), _vec_spec(st.d), st.mod_spec(), st.mod_spec()],
        out_specs=st.row_spec(st.d),
        out_shape=jax.ShapeDtypeStruct((st.b, st.l, st.d), BF16),
        compiler_params=_cp("parallel", "parallel"),
        name="norm_mod",
    )(x, g.reshape(1, -1), sc, sh)


def _norm_router_kernel(x_ref, g_ref, sc_ref, sh_ref, rw_ref, m_ref, aff_ref):
    m = _rms_mod(x_ref[0], g_ref[...], sc_ref[0], sh_ref[0]).astype(BF16)
    m_ref[0] = m
    logits = lax.dot_general(rw_ref[...], m, (((1,), (1,)), ((), ())), preferred_element_type=F32)
    z = logits - jnp.max(logits, axis=0, keepdims=True)
    e = jnp.exp(z)
    aff_ref[0] = e / jnp.sum(e, axis=0, keepdims=True)


def _norm_router_call(st, x, g, sc, sh, router_t):
    n_e = router_t.shape[0]
    return pl.pallas_call(
        _norm_router_kernel,
        grid=(st.b, st.nt),
        in_specs=[st.row_spec(st.d), _vec_spec(st.d), st.mod_spec(), st.mod_spec(),
                  pl.BlockSpec((n_e, st.d), lambda b, i: (0, 0))],
        out_specs=[st.row_spec(st.d), pl.BlockSpec((1, n_e, st.tm), lambda b, i: (b, 0, i))],
        out_shape=[jax.ShapeDtypeStruct((st.b, st.l, st.d), BF16),
                   jax.ShapeDtypeStruct((st.b, n_e, st.l), F32)],
        compiler_params=_cp("parallel", "parallel"),
        name="moe_norm_router",
    )(x, g.reshape(1, -1), sc, sh, router_t)


def _final_norm_kernel(x_ref, g_ref, o_ref):
    x = x_ref[0]
    ms = jnp.mean(x * x, axis=-1, keepdims=True)
    o_ref[0] = x * lax.rsqrt(ms + EPS) * g_ref[...]


def _final_norm_call(st, x, g):
    return pl.pallas_call(
        _final_norm_kernel,
        grid=(st.b, st.n_lat),
        in_specs=[st.row_spec(st.d), _vec_spec(st.d)],
        out_specs=st.row_spec(st.d),
        out_shape=jax.ShapeDtypeStruct((st.b, st.s, st.d), F32),
        compiler_params=_cp("parallel", "parallel"),
        name="final_norm",
    )(x, g.reshape(1, -1))


def _proj_res_kernel(a_ref, w_ref, bias_ref, x_ref, gate_ref, o_ref):
    y = jnp.dot(a_ref[0], w_ref[...], preferred_element_type=F32) + bias_ref[...]
    o_ref[0] = x_ref[0] + gate_ref[0] * y


def _proj_res_call(st, a, w, bias, x, gate):
    k, n = w.shape
    return pl.pallas_call(
        _proj_res_kernel,
        grid=(st.b, st.nt),
        in_specs=[st.row_spec(k), pl.BlockSpec((k, n), lambda b, i: (0, 0)), _vec_spec(n),
                  st.row_spec(n), st.mod_spec()],
        out_specs=st.row_spec(n),
        out_shape=jax.ShapeDtypeStruct((st.b, st.l, n), F32),
        compiler_params=_cp("parallel", "parallel"),
        name="proj_residual",
    )(a, w, bias.reshape(1, -1), x, gate)


def _glu_kernel(a_ref, wa_ref, wb_ref, ba_ref, bb_ref, o_ref):
    a = a_ref[0]
    u = jnp.dot(a, wa_ref[...], preferred_element_type=F32) + ba_ref[...]
    v = jnp.dot(a, wb_ref[...], preferred_element_type=F32) + bb_ref[...]
    o_ref[0] = u * jax.nn.sigmoid(v)


def _glu_call(st, a, w, bias):
    k, n2 = w.shape
    n = n2 // 2
    tn = _divisor(n, 512, 128)
    nj = n // tn
    b2 = bias.reshape(1, -1)
    return pl.pallas_call(
        _glu_kernel,
        grid=(st.b, st.nt, nj),
        in_specs=[st.row_spec(k),
                  pl.BlockSpec((k, tn), lambda b, i, j: (0, j)),
                  pl.BlockSpec((k, tn), lambda b, i, j: (0, j + nj)),
                  pl.BlockSpec((1, tn), lambda b, i, j: (0, j)),
                  pl.BlockSpec((1, tn), lambda b, i, j: (0, j + nj))],
        out_specs=pl.BlockSpec((1, st.tm, tn), lambda b, i, j: (b, i, j)),
        out_shape=jax.ShapeDtypeStruct((st.b, st.l, n), F32),
        compiler_params=_cp("parallel", "parallel", "arbitrary"),
        name="conv_pw1_glu",
    )(a, w, w, b2, b2)


def _halo_specs(st):
    r = st.tm // HALO
    last = st.l // HALO - 1
    prev = pl.BlockSpec((1, HALO, st.d), lambda b, i: (b, jnp.maximum(i * r - 1, 0), 0))
    nxt = pl.BlockSpec((1, HALO, st.d), lambda b, i: (b, jnp.minimum((i + 1) * r, last), 0))
    return prev, nxt


def _segment_masks(st, i):
    is_ctx = i == st.n_lat
    seg_len = jnp.where(is_ctx, st.ctx, st.s)
    base = jnp.where(is_ctx, 0, i * st.tm)
    rows = lax.broadcasted_iota(jnp.int32, (st.tm, 1), 0)
    pos = base + rows
    valid = pos < seg_len
    prev_ok = jnp.logical_and(i != 0, jnp.logical_not(is_ctx))
    next_ok = i < st.n_lat - 1
    return valid, prev_ok, next_ok, pos, seg_len


def _dwconv_kernel(st, width, rc, cc, u_ref, up_ref, un_ref, w_ref, b_ref, g_ref, beta_ref, o_ref,
                   win_ref, sh_ref, acc_ref):
    i = pl.program_id(1)
    valid, prev_ok, next_ok, _, _ = _segment_masks(st, i)
    tm, d = st.tm, st.d
    pad = width // 2
    win_ref[0:HALO, :] = jnp.where(prev_ok, up_ref[0], 0.0)
    win_ref[HALO:HALO + tm, :] = jnp.where(valid, u_ref[0], 0.0)
    win_ref[HALO + tm:, :] = jnp.where(next_ok, un_ref[0], 0.0)
    n_sh = sh_ref.shape[1]

    for c0 in range(0, d, cc):
        for r in range(1, 8):
            sh_ref[r - 1] = win_ref[r:r + n_sh, c0:c0 + cc]

        def row_chunk(rb, carry, c0=c0):
            r0 = pl.multiple_of(rb * rc, rc)
            acc = jnp.zeros((rc, cc), F32)
            for k in range(width):
                off = HALO - pad + k
                a8, res = 8 * (off // 8), off % 8
                if res == 0:
                    tap = win_ref[pl.ds(r0 + a8, rc), c0:c0 + cc]
                else:
                    tap = sh_ref[res - 1, pl.ds(r0 + a8, rc), :]
                acc = acc + tap * w_ref[k:k + 1, c0:c0 + cc]
            acc_ref[pl.ds(r0, rc), c0:c0 + cc] = acc + b_ref[:, c0:c0 + cc]
            return carry

        lax.fori_loop(0, tm // rc, row_chunk, 0)
    y = acc_ref[...]
    mu = jnp.mean(y, axis=-1, keepdims=True)
    yc = y - mu
    var = jnp.mean(yc * yc, axis=-1, keepdims=True)
    z = yc * lax.rsqrt(var + EPS) * g_ref[...] + beta_ref[...]
    o_ref[0] = _silu(z).astype(o_ref.dtype)


def _dwconv_call(st, u, dw_w, dw_b, ln_g, ln_b):
    width = dw_w.shape[0]
    assert width // 2 < HALO
    prev, nxt = _halo_specs(st)
    rc = 32
    cc = _divisor(st.d, 256, 128)
    n_sh = st.tm + 2 * HALO - 8
    kern = functools.partial(_dwconv_kernel, st, width, rc, cc)
    return pl.pallas_call(
        kern,
        grid=(st.b, st.nt),
        in_specs=[st.row_spec(st.d), prev, nxt,
                  pl.BlockSpec((width, st.d), lambda b, i: (0, 0)),
                  _vec_spec(st.d), _vec_spec(st.d), _vec_spec(st.d)],
        out_specs=st.row_spec(st.d),
        out_shape=jax.ShapeDtypeStruct((st.b, st.l, st.d), BF16),
        scratch_shapes=[pltpu.VMEM((st.tm + 2 * HALO, st.d), F32), pltpu.VMEM((7, n_sh, cc), F32),
                        pltpu.VMEM((st.tm, st.d), F32)],
        compiler_params=_cp("parallel", "parallel"),
        name="dwconv_ln_silu",
    )(u, u, u, dw_w, dw_b.reshape(1, -1), ln_g.reshape(1, -1), ln_b.reshape(1, -1))


def _pool_kernel(st, x_ref, xp_ref, xn_ref, g_ref, sc_ref, sh_ref, w_ref, ps_ref, gate_ref, o_ref, win_ref):
    i = pl.program_id(1)
    valid, prev_ok, next_ok, pos, seg_len = _segment_masks(st, i)
    tm, d = st.tm, st.d
    g, sc, sh = g_ref[...], sc_ref[0], sh_ref[0]
    x = x_ref[0]
    n = jnp.where(valid, _rms_mod(x, g, sc, sh), 0.0)
    win_ref[0:HALO, :] = jnp.where(prev_ok, _rms_mod(xp_ref[0], g, sc, sh), 0.0)
    win_ref[HALO:HALO + tm, :] = n
    win_ref[HALO + tm:, :] = jnp.where(next_ok, _rms_mod(xn_ref[0], g, sc, sh), 0.0)
    grp = d // len(POOL_WINDOWS)
    for gi, w in enumerate(POOL_WINDOWS):
        c0 = gi * grp
        tot = jnp.zeros((tm, grp), F32)
        for j in range(-(w // 2), w - w // 2):
            tot = tot + win_ref[HALO + j:HALO + j + tm, c0:c0 + grp]
        cnt = jnp.minimum(pos - w // 2 + w, seg_len) - jnp.maximum(pos - w // 2, 0)
        cnt = jnp.maximum(cnt, 1).astype(F32)
        diff = (tot / cnt - n[:, c0:c0 + grp]).astype(BF16)
        y = jnp.dot(diff, w_ref[gi], preferred_element_type=F32) * ps_ref[:, c0:c0 + grp]
        o_ref[0, :, c0:c0 + grp] = x[:, c0:c0 + grp] + gate_ref[0][:, c0:c0 + grp] * y


def _pool_call(st, x, g, sc, sh, pool_w, pool_scale, gate):
    prev, nxt = _halo_specs(st)
    ng, grp, _ = pool_w.shape
    return pl.pallas_call(
        functools.partial(_pool_kernel, st),
        grid=(st.b, st.nt),
        in_specs=[st.row_spec(st.d), prev, nxt, _vec_spec(st.d), st.mod_spec(), st.mod_spec(),
                  pl.BlockSpec((ng, grp, grp), lambda b, i: (0, 0, 0)), _vec_spec(st.d), st.mod_spec()],
        out_specs=st.row_spec(st.d),
        out_shape=jax.ShapeDtypeStruct((st.b, st.l, st.d), F32),
        scratch_shapes=[pltpu.VMEM((st.tm + 2 * HALO, st.d), F32)],
        compiler_params=_cp("parallel", "parallel"),
        name="pool_mixer",
    )(x, x, x, g.reshape(1, -1), sc, sh, pool_w, pool_scale.reshape(1, -1), gate)


def _rope_tables(st, rot_dim):
    quarter = rot_dim // 4
    rows = st.s // GRID_W
    inv = ROPE_THETA ** (-jnp.arange(quarter, dtype=F32) / quarter)
    row_ang = jnp.arange(rows, dtype=F32)[:, None, None] * inv
    col_ang = jnp.arange(GRID_W, dtype=F32)[None, :, None] * inv
    ang = jnp.concatenate([jnp.broadcast_to(row_ang, (rows, GRID_W, quarter)),
                           jnp.broadcast_to(col_ang, (rows, GRID_W, quarter))], axis=-1).reshape(st.s, 2 * quarter)
    cos, sin = jnp.cos(ang), jnp.sin(ang)
    c_tab = jnp.concatenate([cos, cos], axis=-1)
    s_tab = jnp.concatenate([-sin, sin], axis=-1)
    c_tab = jnp.concatenate([c_tab, jnp.ones((st.ctx, rot_dim), F32)], axis=0)
    s_tab = jnp.concatenate([s_tab, jnp.zeros((st.ctx, rot_dim), F32)], axis=0)
    return c_tab, s_tab


def _tab_spec(st, width):
    return pl.BlockSpec((st.tm, width), lambda b, i, *_: (i, 0))


def _folded_rope(r, cs):
    t = r * cs
    t = t + pltpu.roll(t, 64, axis=1)
    lane = lax.broadcasted_iota(jnp.int32, t.shape, 1)
    return jnp.where(lane < MLA_ROPE, t, 0.0)


def _mla_down_kernel(q_rank, kv_rank, n_ref, w_ref, qn_ref, kvn_ref, cs_ref, cq_ref, ckv_ref, kr_ref):
    acc = jnp.dot(n_ref[0], w_ref[...], preferred_element_type=F32)
    cq = acc[:, :q_rank]
    cq_ref[0] = (cq * lax.rsqrt(jnp.mean(cq * cq, axis=-1, keepdims=True) + EPS) * qn_ref[...]).astype(BF16)
    ckv = acc[:, q_rank:q_rank + kv_rank]
    ckv_ref[0] = (ckv * lax.rsqrt(jnp.mean(ckv * ckv, axis=-1, keepdims=True) + EPS) * kvn_ref[...]).astype(BF16)
    kr_ref[0] = _folded_rope(acc[:, q_rank + kv_rank:], cs_ref[...]).astype(BF16)


def _mla_down_call(st, n, w_cat, q_norm, kv_norm, cs_tab):
    k, ncols = w_cat.shape
    q_rank, kv_rank = q_norm.shape[0], kv_norm.shape[0]
    assert ncols == q_rank + kv_rank + 128 and q_rank % 128 == 0 and kv_rank % 128 == 0
    return pl.pallas_call(
        functools.partial(_mla_down_kernel, q_rank, kv_rank),
        grid=(st.b, st.nt),
        in_specs=[st.row_spec(k), pl.BlockSpec((k, ncols), lambda b, i: (0, 0)),
                  _vec_spec(q_rank), _vec_spec(kv_rank), _tab_spec(st, 128)],
        out_specs=[st.row_spec(q_rank), st.row_spec(kv_rank), st.row_spec(128)],
        out_shape=[jax.ShapeDtypeStruct((st.b, st.l, q_rank), BF16),
                   jax.ShapeDtypeStruct((st.b, st.l, kv_rank), BF16),
                   jax.ShapeDtypeStruct((st.b, st.l, 128), BF16)],
        compiler_params=_cp("parallel", "parallel"),
        name="mla_down",
    )(n, w_cat, q_norm.reshape(1, -1), kv_norm.reshape(1, -1), cs_tab)


def _mla_uq_kernel(heads, qscale, cq_ref, w_ref, cs_ref, qt_ref):
    cq, cs = cq_ref[0], cs_ref[...]
    for h in range(heads):
        acc = jnp.dot(cq, w_ref[:, h * MLA_QK_PAD:(h + 1) * MLA_QK_PAD], preferred_element_type=F32)
        q = jnp.concatenate([acc[:, :MLA_NOPE], _folded_rope(acc[:, MLA_NOPE:], cs)], axis=1) * qscale
        qt_ref[0, h] = q.T.astype(BF16)


def _mla_uq_call(st, cq, w_ext, cs_tab, heads, qscale):
    k = cq.shape[-1]
    return pl.pallas_call(
        functools.partial(_mla_uq_kernel, heads, qscale),
        grid=(st.b, st.nt),
        in_specs=[st.row_spec(k), pl.BlockSpec((k, heads * MLA_QK_PAD), lambda b, i: (0, 0)), _tab_spec(st, 128)],
        out_specs=pl.BlockSpec((1, heads, MLA_QK_PAD, st.tm), lambda b, i: (b, 0, 0, i)),
        out_shape=jax.ShapeDtypeStruct((st.b, heads, MLA_QK_PAD, st.l), BF16),
        compiler_params=_cp("parallel", "parallel"),
        name="mla_up_q",
    )(cq, w_ext, cs_tab)


def _mla_ukv_kernel(heads, ckv_ref, w_ref, kr_ref, k_ref, vt_ref):
    ckv, kr = ckv_ref[0], kr_ref[0]
    hw = MLA_NOPE + MLA_V
    for h in range(heads):
        acc = jnp.dot(ckv, w_ref[:, h * hw:(h + 1) * hw], preferred_element_type=F32)
        k_ref[0, h, :, :MLA_NOPE] = acc[:, :MLA_NOPE].astype(BF16)
        k_ref[0, h, :, MLA_NOPE:] = kr
        vt_ref[0, h] = acc[:, MLA_NOPE:].T.astype(BF16)


def _mla_ukv_call(st, ckv, w_ukv, kr, heads):
    k = ckv.shape[-1]
    hw = MLA_NOPE + MLA_V
    return pl.pallas_call(
        functools.partial(_mla_ukv_kernel, heads),
        grid=(st.b, st.nt),
        in_specs=[st.row_spec(k), pl.BlockSpec((k, heads * hw), lambda b, i: (0, 0)), st.row_spec(128)],
        out_specs=[pl.BlockSpec((1, heads, st.tm, MLA_QK_PAD), lambda b, i: (b, 0, i, 0)),
                   pl.BlockSpec((1, heads, MLA_V, st.tm), lambda b, i: (b, 0, 0, i))],
        out_shape=[jax.ShapeDtypeStruct((st.b, heads, st.l, MLA_QK_PAD), BF16),
                   jax.ShapeDtypeStruct((st.b, heads, MLA_V, st.l), BF16)],
        compiler_params=_cp("parallel", "parallel"),
        name="mla_up_kv",
    )(ckv, w_ukv, kr)


def _softmax_chunk_t(st, tk, s_ref, j, vt, m_ref, l_ref, acc_ref, masked):
    s = s_ref[...]
    if masked:
        row = j * tk + lax.broadcasted_iota(jnp.int32, s.shape, 0)
        s = jnp.where(row >= st.s, s, NEG_BIG)
    m_prev = m_ref[...]
    m_new = jnp.maximum(m_prev, jnp.max(s, axis=0, keepdims=True))
    alpha = jnp.exp2(m_prev - m_new)
    p = jnp.exp2(s - m_new)
    l_ref[...] = alpha * l_ref[...] + jnp.sum(p, axis=0, keepdims=True)
    acc_ref[...] = alpha * acc_ref[...] + jnp.dot(vt, p.astype(BF16), preferred_element_type=F32)
    m_ref[...] = m_new


def _pipelined_key_loop(st, tq, tk, qi, qk, qk_next, proc, bufs0, bufs1, bufs2):
    n_lat_q = st.s // tq
    nch = st.l // tk
    carry_first_chunk = nch % 2 == 1 and nch >= 3

    def pair(jj, carry):
        qk(2 * jj + 1, bufs1)
        proc(2 * jj, bufs0, False)
        qk(2 * jj + 2, bufs0)
        proc(2 * jj + 1, bufs1, False)
        return carry

    if carry_first_chunk:
        @pl.when(qi == 0)
        def _():
            qk(0, bufs2)

        @pl.when(qi < n_lat_q)
        def _():
            qk(1, bufs1)
            proc(0, bufs2, False)
            qk(2, bufs0)
            proc(1, bufs1, False)
            lax.fori_loop(1, (nch - 1) // 2, pair, 0)
            qk_next(bufs2)
            proc(nch - 1, bufs0, False)
    else:
        @pl.when(qi < n_lat_q)
        def _():
            qk(0, bufs0)
            lax.fori_loop(0, (nch - 1) // 2, pair, 0)
            if nch % 2 == 1:
                proc(nch - 1, bufs0, False)
            else:
                qk(nch - 1, bufs1)
                proc(nch - 2, bufs0, False)
                proc(nch - 1, bufs1, False)

    @pl.when(qi == n_lat_q)
    def _():
        for j in range(st.s // tk, nch):
            qk(j, bufs0)
            proc(j, bufs0, True)


def _chunk(j, tk):
    return pl.ds(pl.multiple_of(j * tk, tk), tk)


def _mla_attn_kernel(st, tq, tk, qt_ref, qt_next_ref, k_ref, vt_ref, o_ref, s0_ref, s1_ref, s2_ref,
                     m_ref, l_ref, acc_ref):
    m_ref[...] = jnp.full_like(m_ref, NEG_BIG)
    l_ref[...] = jnp.zeros_like(l_ref)
    acc_ref[...] = jnp.zeros_like(acc_ref)

    def scores(j, q_ref, s_ref):
        s_ref[...] = jnp.dot(k_ref[0, 0, _chunk(j, tk), :], q_ref[0, 0], preferred_element_type=F32)

    def proc(j, s_ref, masked):
        _softmax_chunk_t(st, tk, s_ref, j, vt_ref[0, 0, :, _chunk(j, tk)], m_ref, l_ref, acc_ref, masked)

    _pipelined_key_loop(st, tq, tk, pl.program_id(2),
                        lambda j, s_ref: scores(j, qt_ref, s_ref), lambda s_ref: scores(0, qt_next_ref, s_ref),
                        proc, s0_ref, s1_ref, s2_ref)
    o_ref[0] = (acc_ref[...] / l_ref[...]).T.astype(o_ref.dtype)


def _attn_tiles(st, tq_target):
    tq = _divisor(st.s, tq_target, 128)
    assert st.ctx <= tq
    tk = _divisor(st.l, 1536, 256 if st.l % 256 == 0 else 128)
    return tq, tk


def _mla_attn_call(st, qt, k, vt, heads):
    tq, tk = _attn_tiles(st, 1024)
    last = st.s // tq
    return pl.pallas_call(
        functools.partial(_mla_attn_kernel, st, tq, tk),
        grid=(st.b, heads, last + 1),
        in_specs=[pl.BlockSpec((1, 1, MLA_QK_PAD, tq), lambda b, h, i: (b, h, 0, i)),
                  pl.BlockSpec((1, 1, MLA_QK_PAD, tq), lambda b, h, i: (b, h, 0, jnp.minimum(i + 1, last))),
                  pl.BlockSpec((1, 1, st.l, MLA_QK_PAD), lambda b, h, i: (b, h, 0, 0)),
                  pl.BlockSpec((1, 1, MLA_V, st.l), lambda b, h, i: (b, h, 0, 0))],
        out_specs=pl.BlockSpec((1, tq, MLA_V), lambda b, h, i: (b, i, h)),
        out_shape=jax.ShapeDtypeStruct((st.b, st.l, heads * MLA_V), BF16),
        scratch_shapes=[pltpu.VMEM((tk, tq), F32)] * 3
        + [pltpu.VMEM((1, tq), F32), pltpu.VMEM((1, tq), F32), pltpu.VMEM((MLA_V, tq), F32)],
        compiler_params=_cp("arbitrary", "arbitrary", "arbitrary"),
        name="mla_attention",
    )(qt, qt, k, vt)


def _diff_proj_kernel(heads, mode, qscale, n_ref, w_ref, c_ref, s_ref, o_ref):
    hd = DIFF_HEAD_DIM
    acc = jnp.dot(n_ref[0], w_ref[...], preferred_element_type=F32)
    if mode != "v":
        c, s = c_ref[...], s_ref[...]
    for h in range(heads):
        if mode == "v":
            o_ref[0, h] = acc[:, 2 * h * hd:2 * (h + 1) * hd].T.astype(BF16)
            continue
        halves = []
        for n in range(2):
            xh = acc[:, (2 * h + n) * hd:(2 * h + n + 1) * hd]
            halves.append(xh * c + pltpu.roll(xh, hd // 2, axis=1) * s)
        if mode == "k":
            o_ref[0, :, 2 * h * hd:(2 * h + 1) * hd] = halves[0].astype(BF16)
            o_ref[0, :, (2 * h + 1) * hd:2 * (h + 1) * hd] = halves[1].astype(BF16)
        else:
            o_ref[0, h] = (jnp.concatenate(halves, axis=1) * qscale).T.astype(BF16)


def _diff_proj_call(st, n, w_qkv, c_tab, s_tab, heads, mode, qscale=1.0):
    k = w_qkv.shape[0]
    hw = 2 * DIFF_HEAD_DIM
    third = {"q": 0, "k": 1, "v": 2}[mode]
    if mode == "k":
        out_spec = st.row_spec(heads * hw)
        out_shape = jax.ShapeDtypeStruct((st.b, st.l, heads * hw), BF16)
    else:
        out_spec = pl.BlockSpec((1, heads, hw, st.tm), lambda b, i: (b, 0, 0, i))
        out_shape = jax.ShapeDtypeStruct((st.b, heads, hw, st.l), BF16)
    return pl.pallas_call(
        functools.partial(_diff_proj_kernel, heads, mode, qscale),
        grid=(st.b, st.nt),
        in_specs=[st.row_spec(k), pl.BlockSpec((k, heads * hw), lambda b, i: (0, third)),
                  _tab_spec(st, DIFF_HEAD_DIM), _tab_spec(st, DIFF_HEAD_DIM)],
        out_specs=out_spec,
        out_shape=out_shape,
        compiler_params=_cp("parallel", "parallel"),
        name="diff_proj_" + mode,
    )(n, w_qkv, c_tab, s_tab)


def _diff_attn_kernel(st, tq, tk, lam_init, qt_ref, qt_next_ref, k_ref, vt_ref, lq1_ref, lk1_ref, lq2_ref, lk2_ref,
                      sg_ref, o_ref, sa0, sb0, sa1, sb1, sa2, sb2, m1, l1, a1, m2, l2, a2):
    hd = DIFF_HEAD_DIM
    for m_ref, l_ref, a_ref in ((m1, l1, a1), (m2, l2, a2)):
        m_ref[...] = jnp.full_like(m_ref, NEG_BIG)
        l_ref[...] = jnp.zeros_like(l_ref)
        a_ref[...] = jnp.zeros_like(a_ref)

    def scores(j, q_ref, bufs):
        kc = k_ref[0, _chunk(j, tk), :]
        bufs[0][...] = jnp.dot(kc[:, :hd], q_ref[0, 0, :hd, :], preferred_element_type=F32)
        bufs[1][...] = jnp.dot(kc[:, hd:], q_ref[0, 0, hd:, :], preferred_element_type=F32)

    def proc(j, bufs, masked):
        vt = vt_ref[0, 0, :, _chunk(j, tk)]
        _softmax_chunk_t(st, tk, bufs[0], j, vt, m1, l1, a1, masked)
        _softmax_chunk_t(st, tk, bufs[1], j, vt, m2, l2, a2, masked)

    _pipelined_key_loop(st, tq, tk, pl.program_id(2),
                        lambda j, bufs: scores(j, qt_ref, bufs), lambda bufs: scores(0, qt_next_ref, bufs),
                        proc, (sa0, sb0), (sa1, sb1), (sa2, sb2))
    lam = (jnp.exp(jnp.sum(lq1_ref[...] * lk1_ref[...], axis=-1, keepdims=True))
           - jnp.exp(jnp.sum(lq2_ref[...] * lk2_ref[...], axis=-1, keepdims=True)) + lam_init)
    o = a1[...] / l1[...] - lam * (a2[...] / l2[...])
    o = o * lax.rsqrt(jnp.mean(o * o, axis=0, keepdims=True) + EPS) * sg_ref[...]
    o_ref[0] = (o * (1.0 - lam_init)).T.astype(o_ref.dtype)


def _diff_attn_call(st, qt, k, vt, lq1, lk1, lq2, lk2, subln_g, heads, lam_init):
    tq, tk = _attn_tiles(st, 512)
    hw = 2 * DIFF_HEAD_DIM
    vec = lambda a: a.reshape(1, -1)
    last = st.s // tq
    return pl.pallas_call(
        functools.partial(_diff_attn_kernel, st, tq, tk, lam_init),
        grid=(st.b, heads, last + 1),
        in_specs=[pl.BlockSpec((1, 1, hw, tq), lambda b, h, i: (b, h, 0, i)),
                  pl.BlockSpec((1, 1, hw, tq), lambda b, h, i: (b, h, 0, jnp.minimum(i + 1, last))),
                  pl.BlockSpec((1, st.l, hw), lambda b, h, i: (b, 0, h), pipeline_mode=pl.Buffered(1)),
                  pl.BlockSpec((1, 1, hw, st.l), lambda b, h, i: (b, h, 0, 0), pipeline_mode=pl.Buffered(1)),
                  _vec_spec(DIFF_HEAD_DIM), _vec_spec(DIFF_HEAD_DIM), _vec_spec(DIFF_HEAD_DIM),
                  _vec_spec(DIFF_HEAD_DIM), pl.BlockSpec((hw, 1), lambda *_: (0, 0))],
        out_specs=pl.BlockSpec((1, tq, hw), lambda b, h, i: (b, i, h)),
        out_shape=jax.ShapeDtypeStruct((st.b, st.l, heads * hw), BF16),
        scratch_shapes=[pltpu.VMEM((tk, tq), F32)] * 6
        + [pltpu.VMEM((1, tq), F32), pltpu.VMEM((1, tq), F32), pltpu.VMEM((hw, tq), F32)] * 2,
        compiler_params=_cp("arbitrary", "arbitrary", "arbitrary"),
        name="diff_attention",
    )(qt, qt, k, vt, vec(lq1), vec(lk1), vec(lq2), vec(lk2), subln_g.reshape(-1, 1))


W_CHUNKS = 16


def _expert_kernel(tm, n_local, nt, layer, e0, idx_hbm, x_ref, wg_hbm, wu_hbm, wd_hbm, g_ref, gate_ref, acc_in,
                   acc_hbm, idx_smem, rows_ref, wg_bf, wu_bf, wd_bf, stage_g, stage_u, stage_d,
                   sem_idx, sem_gather, sem_scatter, sem_w):
    del acc_in
    e_local, tile = pl.program_id(0), pl.program_id(1)
    n_steps = n_local * nt
    first_of_expert = tile == 0
    k = e_local * nt + tile
    slot = lax.rem(k, 2)
    wslot = lax.rem(e_local, 2)
    rows_g, rows_d = wg_bf.shape[1] // W_CHUNKS, wd_bf.shape[1] // W_CHUNKS

    def w_copies(expert, c, s):
        cg, cd = pl.ds(pl.multiple_of(c * rows_g, rows_g), rows_g), pl.ds(pl.multiple_of(c * rows_d, rows_d), rows_d)
        return (pltpu.make_async_copy(wg_hbm.at[layer, expert, cg, :], stage_g.at[s], sem_w.at[s]),
                pltpu.make_async_copy(wu_hbm.at[layer, expert, cg, :], stage_u.at[s], sem_w.at[s]),
                pltpu.make_async_copy(wd_hbm.at[layer, expert, cd, :], stage_d.at[s], sem_w.at[s]))

    def w_start(expert, c, s):
        for cp in w_copies(expert, c, s):
            cp.start()

    def w_finish(expert, c, s, dst):
        for cp in w_copies(expert, c, s):
            cp.wait()
        cg, cd = pl.ds(pl.multiple_of(c * rows_g, rows_g), rows_g), pl.ds(pl.multiple_of(c * rows_d, rows_d), rows_d)
        wg_bf[dst, cg, :] = stage_g[s].astype(BF16)
        wu_bf[dst, cg, :] = stage_u[s].astype(BF16)
        wd_bf[dst, cd, :] = stage_d[s].astype(BF16)

    def w_stream(expert, chunks, dst, first_started):
        slots = stage_g.shape[0]
        for n in range(1 if first_started else 0, min(slots, len(chunks))):
            w_start(expert, chunks[n], n)
        for n, c in enumerate(chunks):
            w_finish(expert, c, n % slots, dst)
            if n + slots < len(chunks):
                w_start(expert, chunks[n + slots], n % slots)

    @pl.when(k == 0)
    def _():
        w_stream(e0, list(range(W_CHUNKS)), 0, False)

    per_tile = W_CHUNKS // nt
    next_chunks = [tile * per_tile + n for n in range(per_tile)]
    has_next = e_local + 1 < n_local

    @pl.when(has_next)
    def _():
        w_start(e0 + e_local + 1, next_chunks[0], 0)

    def idx_copy(step, s):
        return pltpu.make_async_copy(idx_hbm.at[step], idx_smem.at[s], sem_idx.at[s])

    def row_copy(s, r, sem, to_hbm):
        hbm, vmem = acc_hbm.at[pl.ds(idx_smem[s, r], 1), :], rows_ref.at[s, pl.ds(r, 1), :]
        return pltpu.make_async_copy(vmem, hbm, sem) if to_hbm else pltpu.make_async_copy(hbm, vmem, sem)

    def for_this_slot(body):
        for s in (0, 1):
            pl.when(slot == s)(functools.partial(body, s))

    def all_rows(sem):
        return pltpu.make_async_copy(acc_hbm.at[pl.ds(0, tm), :], rows_ref.at[0], sem)

    @pl.when(k == 0)
    def _():
        idx_copy(0, 0).start()

    idx_copy(k, slot).wait()

    @pl.when(k + 1 < n_steps)
    def _():
        idx_copy(k + 1, 1 - slot).start()

    @pl.when(jnp.logical_and(k > 0, first_of_expert))
    def _():
        all_rows(sem_scatter).wait()

    def gather(s):
        for r in range(tm):
            row_copy(s, r, sem_gather, False).start()

    for_this_slot(gather)

    x = x_ref[0]
    a = jnp.dot(x, wg_bf[wslot], preferred_element_type=F32)
    u = jnp.dot(x, wu_bf[wslot], preferred_element_type=F32)
    h = (_silu(a) * u).astype(BF16)
    y = jnp.dot(h, wd_bf[wslot], preferred_element_type=F32) * g_ref[0]

    @pl.when(has_next)
    def _():
        w_stream(e0 + e_local + 1, next_chunks, 1 - wslot, True)

    all_rows(sem_gather).wait()

    @pl.when(jnp.logical_not(first_of_expert))
    def _():
        all_rows(sem_scatter).wait()

    def accumulate_and_write_back(s):
        rows_ref[s] = rows_ref[s] + gate_ref[0] * y
        for r in range(tm):
            row_copy(s, r, sem_scatter, True).start()

    for_this_slot(accumulate_and_write_back)

    @pl.when(k == n_steps - 1)
    def _():
        all_rows(sem_scatter).wait()


def _expert_call(xs, wg, wu, wd, layer, gates, idx, acc, gate, tm, gate_row0, e0):
    n_e, rows, d = xs.shape
    ff = wg.shape[-1]
    nt = rows // tm
    assert W_CHUNKS % nt == 0 and d % (16 * W_CHUNKS) == 0 and ff % (16 * W_CHUNKS) == 0
    tiles_per_sample = nt // (gate.shape[0] // 2)
    slots = max(2, W_CHUNKS // nt)
    params = pltpu.CompilerParams(dimension_semantics=("arbitrary", "arbitrary"), vmem_limit_bytes=VMEM_LIMIT_BYTES,
                                  disable_bounds_checks=True)
    hbm = pl.BlockSpec(memory_space=pl.ANY)
    return pl.pallas_call(
        functools.partial(_expert_kernel, tm, n_e, nt, layer, e0),
        grid=(n_e, nt),
        in_specs=[hbm,
                  pl.BlockSpec((1, tm, d), lambda e, i: (e, i, 0)),
                  hbm, hbm, hbm,
                  pl.BlockSpec((1, tm, 1), lambda e, i: (e, i, 0)),
                  pl.BlockSpec((1, 1, d), lambda e, i: ((i // tiles_per_sample) * 2 + gate_row0, 0, 0)),
                  hbm],
        out_specs=hbm,
        out_shape=jax.ShapeDtypeStruct(acc.shape, F32),
        input_output_aliases={7: 0},
        scratch_shapes=[pltpu.SMEM((2, tm), jnp.int32), pltpu.VMEM((2, tm, d), F32),
                        pltpu.VMEM((2, d, ff), BF16), pltpu.VMEM((2, d, ff), BF16), pltpu.VMEM((2, ff, d), BF16),
                        pltpu.VMEM((slots, d // W_CHUNKS, ff), F32), pltpu.VMEM((slots, d // W_CHUNKS, ff), F32),
                        pltpu.VMEM((slots, ff // W_CHUNKS, d), F32),
                        pltpu.SemaphoreType.DMA((2,)), pltpu.SemaphoreType.DMA, pltpu.SemaphoreType.DMA,
                        pltpu.SemaphoreType.DMA((slots,))],
        compiler_params=params,
        name="moe_experts",
    )(idx.reshape(n_e * nt, tm), xs, wg, wu, wd, gates, gate, acc)


def _routed_experts(st, m, aff_seg, offset, cap, n_groups, acc, wg, wu, wd, layer, gate, gate_row0):
    n_e = aff_seg.shape[1]
    g_sel, idx = lax.top_k(aff_seg, cap)
    idx = idx + offset
    bidx = jnp.arange(st.b)[:, None, None]
    rows = (idx + bidx * st.l).astype(jnp.int32)
    per = n_e // n_groups
    tm = _divisor(cap, 256, 16)
    for e0 in range(0, n_e, per):
        sl = slice(e0, e0 + per)
        xs = jnp.swapaxes(m[bidx, idx[:, sl]], 0, 1).reshape(per, st.b * cap, st.d)
        gates = jnp.swapaxes(g_sel[:, sl], 0, 1).reshape(per, st.b * cap, 1)
        rows_g = jnp.swapaxes(rows[:, sl], 0, 1).reshape(per, st.b * cap)
        acc = _expert_call(xs, wg, wu, wd, layer, gates, rows_g, acc, gate, tm, gate_row0, e0)
    return acc


def _moe(st, x, g, sc, sh, gate, router_w, wg, wu, wd, layer, with_ctx):
    n_e = router_w.shape[-1]
    m, aff = _norm_router_call(st, x, g, sc, sh, router_w.T.astype(BF16))
    acc = x.reshape(st.b * st.l, st.d)
    cap_l = max(1, EC_FACTOR * st.s // n_e)
    acc = _routed_experts(st, m, aff[:, :, :st.s], 0, cap_l, 2 if n_e % 2 == 0 else 1, acc, wg, wu, wd, layer,
                          gate, 0)
    if with_ctx:
        cap_c = max(1, EC_FACTOR * st.ctx // n_e)
        acc = _routed_experts(st, m, aff[:, :, st.s:], st.s, cap_c, 1, acc, wg, wu, wd, layer, gate, 1)
    return acc.reshape(st.b, st.l, st.d)


def _swap_halves_cols(w, width):
    k, n = w.shape
    w = w.reshape(k, n // width, 2, width // 2)
    return w[:, :, ::-1, :].reshape(k, n)


def kernel(x, c, ctx, c_ctx, ada_w, ada_b, norm_g, final_g, conv_pw1_w, conv_pw1_b, conv_dw_w, conv_dw_b, conv_ln_g, conv_ln_b, conv_pw2_w, conv_pw2_b, mla_w_dq, mla_q_norm, mla_w_uq, mla_w_dkv, mla_kv_norm, mla_w_ukv, mla_w_o, pool_w, pool_scale, diff_w_qkv, diff_lq1, diff_lk1, diff_lq2, diff_lk2, diff_subln_g, diff_w_o, moe_router, moe_w_gate, moe_w_up, moe_w_down):
    b, s, d = x.shape
    n_ctx = ctx.shape[1]
    depth = ada_w.shape[0]
    st = _Stream(b, s, n_ctx, d, _divisor(s, 512, max(n_ctx, 128)))
    xs = jnp.concatenate([x, ctx], axis=1)

    c8 = jnp.zeros((8, d), F32).at[:b].set(c).at[b].set(c_ctx)
    mod = _mod_call(c8, ada_w, ada_b)
    order = jnp.stack([jnp.arange(b), jnp.full((b,), b)], axis=1).reshape(-1)
    mod = mod[:, order, :].reshape(depth, 2 * b, 6, d)
    mods = [[mod[i, :, k, :].reshape(2 * b, 1, d) for k in range(6)] for i in range(depth)]

    heads = mla_w_uq.shape[1] // (MLA_NOPE + MLA_ROPE)
    q_rank, kv_rank = mla_q_norm.shape[0], mla_kv_norm.shape[0]
    diff_heads = diff_w_o.shape[0] // (2 * DIFF_HEAD_DIM)
    zeros_d = jnp.zeros((d,), F32)

    for i in range(depth):
        kind = i % 4
        need_ctx_out = i < depth - 1
        sh1, sc1, gt1, sh2, sc2, gt2 = mods[i]
        if kind == 0:
            n = _norm_call(st, xs, norm_g[i, 0], sc1, sh1)
            u = _glu_call(st, n, conv_pw1_w.astype(BF16), conv_pw1_b)
            v = _dwconv_call(st, u, conv_dw_w, conv_dw_b, conv_ln_g, conv_ln_b)
            xs = _proj_res_call(st, v, conv_pw2_w.astype(BF16), conv_pw2_b, xs, gt1)
        elif kind == 1:
            n = _norm_call(st, xs, norm_g[i, 0], sc1, sh1)
            c_tab, s_tab = _rope_tables(st, MLA_ROPE)
            cs_tab = jnp.concatenate([c_tab, s_tab], axis=-1)
            w_kr = mla_w_dkv[:, kv_rank:]
            w_cat = jnp.concatenate([mla_w_dq, mla_w_dkv[:, :kv_rank], w_kr, _swap_halves_cols(w_kr, MLA_ROPE)],
                                    axis=1).astype(BF16)
            cq, ckv, kr = _mla_down_call(st, n, w_cat, mla_q_norm, mla_kv_norm, cs_tab)
            w_uq = mla_w_uq.reshape(q_rank, heads, MLA_NOPE + MLA_ROPE)
            w_uq_rope = w_uq[:, :, MLA_NOPE:].reshape(q_rank, heads * MLA_ROPE)
            w_uq_ext = jnp.concatenate(
                [w_uq, _swap_halves_cols(w_uq_rope, MLA_ROPE).reshape(q_rank, heads, MLA_ROPE)], axis=-1)
            qscale = (MLA_NOPE + MLA_ROPE) ** -0.5 * LOG2E
            qt = _mla_uq_call(st, cq, w_uq_ext.reshape(q_rank, heads * MLA_QK_PAD).astype(BF16), cs_tab, heads, qscale)
            k, vt = _mla_ukv_call(st, ckv, mla_w_ukv.astype(BF16), kr, heads)
            o = _mla_attn_call(st, qt, k, vt, heads)
            xs = _proj_res_call(st, o, mla_w_o.astype(BF16), zeros_d, xs, gt1)
        elif kind == 2:
            xs = _pool_call(st, xs, norm_g[i, 0], sc1, sh1, pool_w.astype(BF16), pool_scale, gt1)
        else:
            n = _norm_call(st, xs, norm_g[i, 0], sc1, sh1)
            c_tab, s_tab = _rope_tables(st, DIFF_HEAD_DIM)
            w_qkv = diff_w_qkv.astype(BF16)
            qt = _diff_proj_call(st, n, w_qkv, c_tab, s_tab, diff_heads, "q", DIFF_HEAD_DIM ** -0.5 * LOG2E)
            k = _diff_proj_call(st, n, w_qkv, c_tab, s_tab, diff_heads, "k")
            vt = _diff_proj_call(st, n, w_qkv, c_tab, s_tab, diff_heads, "v")
            lam_init = 0.8 - 0.6 * math.exp(-0.3 * i)
            o = _diff_attn_call(st, qt, k, vt, diff_lq1, diff_lk1, diff_lq2, diff_lk2, diff_subln_g, diff_heads,
                                lam_init)
            xs = _proj_res_call(st, o, diff_w_o.astype(BF16), zeros_d, xs, gt1)
        xs = _moe(st, xs, norm_g[i, 1], sc2, sh2, gt2, moe_router[i], moe_w_gate, moe_w_up, moe_w_down, i,
                  need_ctx_out)
    return _final_norm_call(st, xs, final_g)
```

```python
import functools
import math

import jax
import jax.numpy as jnp
from jax import lax
from jax.experimental import pallas as pl
from jax.experimental.pallas import tpu as pltpu

F32 = jnp.float32
BF16 = jnp.bfloat16

GRID_W = 64
EPS = 1e-6
ROPE_THETA = 10000.0
MLA_NOPE = 128
MLA_ROPE = 64
MLA_V = 128
MLA_QK_PAD = 256
DIFF_HEAD_DIM = 128
POOL_WINDOWS = (2, 4, 8, 16)
EC_FACTOR = 2
HALO = 16
NEG_BIG = -1e30
LOG2E = 1.4426950408889634

VMEM_LIMIT_BYTES = 56 * 1024 * 1024


def _cp(*sem):
    return pltpu.CompilerParams(dimension_semantics=sem, vmem_limit_bytes=VMEM_LIMIT_BYTES)


def _divisor(n, target, mult):
    best = None
    d = mult
    while d <= min(n, target):
        if n % d == 0:
            best = d
        d += mult
    return best if best is not None else n


def _silu(v):
    return v * jax.nn.sigmoid(v)


def _rms_mod(x, g, sc, sh):
    ms = jnp.mean(x * x, axis=-1, keepdims=True)
    return (x * lax.rsqrt(ms + EPS) * g) * (1.0 + sc) + sh


def _mod_kernel(c_ref, w_ref, b_ref, o_ref):
    s = _silu(c_ref[...]).astype(BF16)
    o_ref[0] = jnp.dot(s, w_ref[0].astype(BF16), preferred_element_type=F32) + b_ref[0]


def _mod_call(c8, ada_w, ada_b):
    depth, d, n = ada_w.shape
    tn = _divisor(n, 1024, 128)
    return pl.pallas_call(
        _mod_kernel,
        grid=(depth, n // tn),
        in_specs=[pl.BlockSpec((8, d), lambda l, j: (0, 0)),
                  pl.BlockSpec((1, d, tn), lambda l, j: (l, 0, j)),
                  pl.BlockSpec((1, 1, tn), lambda l, j: (l, 0, j))],
        out_specs=pl.BlockSpec((1, 8, tn), lambda l, j: (l, 0, j)),
        out_shape=jax.ShapeDtypeStruct((depth, 8, n), F32),
        compiler_params=_cp("arbitrary", "arbitrary"),
        name="adaln_mod",
    )(c8, ada_w, ada_b.reshape(depth, 1, n))


class _Stream:
    def __init__(self, b, s, ctx, d, tm):
        assert s % tm == 0 and ctx <= tm and ctx % HALO == 0 and tm % HALO == 0
        self.b, self.s, self.ctx, self.d, self.tm = b, s, ctx, d, tm
        self.l = s + ctx
        self.n_lat = s // tm
        self.nt = self.n_lat + 1

    def row_spec(self, width, col=None):
        if col is None:
            return pl.BlockSpec((1, self.tm, width), lambda b, i, *_: (b, i, 0))
        return pl.BlockSpec((1, self.tm, width), lambda b, i, *_: (b, i, col))

    def mod_spec(self):
        n_lat = self.n_lat
        return pl.BlockSpec((1, 1, self.d), lambda b, i, *_: (b * 2 + i // n_lat, 0, 0))


def _vec_spec(width):
    return pl.BlockSpec((1, width), lambda *_: (0, 0))


def _norm_kernel(x_ref, g_ref, sc_ref, sh_ref, o_ref):
    o_ref[0] = _rms_mod(x_ref[0], g_ref[...], sc_ref[0], sh_ref[0]).astype(o_ref.dtype)


def _norm_call(st, x, g, sc, sh):
    return pl.pallas_call(
        _norm_kernel,
        grid=(st.b, st.nt),
        in_specs=[st.row_spec(st.d), _vec_spec(st.d), st.mod_spec(), st.mod_spec()],
        out_specs=st.row_spec(st.d),
        out_shape=jax.ShapeDtypeStruct((st.b, st.l, st.d), BF16),
        compiler_params=_cp("parallel", "parallel"),
        name="norm_mod",
    )(x, g.reshape(1, -1), sc, sh)


def _norm_router_kernel(x_ref, g_ref, sc_ref, sh_ref, rw_ref, m_ref, aff_ref):
    m = _rms_mod(x_ref[0], g_ref[...], sc_ref[0], sh_ref[0]).astype(BF16)
    m_ref[0] = m
    logits = lax.dot_general(rw_ref[...], m, (((1,), (1,)), ((), ())), preferred_element_type=F32)
    z = logits - jnp.max(logits, axis=0, keepdims=True)
    e = jnp.exp(z)
    aff_ref[0] = e / jnp.sum(e, axis=0, keepdims=True)


def _norm_router_call(st, x, g, sc, sh, router_t):
    n_e = router_t.shape[0]
    return pl.pallas_call(
        _norm_router_kernel,
        grid=(st.b, st.nt),
        in_specs=[st.row_spec(st.d), _vec_spec(st.d), st.mod_spec(), st.mod_spec(),
                  pl.BlockSpec((n_e, st.d), lambda b, i: (0, 0))],
        out_specs=[st.row_spec(st.d), pl.BlockSpec((1, n_e, st.tm), lambda b, i: (b, 0, i))],
        out_shape=[jax.ShapeDtypeStruct((st.b, st.l, st.d), BF16),
                   jax.ShapeDtypeStruct((st.b, n_e, st.l), F32)],
        compiler_params=_cp("parallel", "parallel"),
        name="moe_norm_router",
    )(x, g.reshape(1, -1), sc, sh, router_t)


def _final_norm_kernel(x_ref, g_ref, o_ref):
    x = x_ref[0]
    ms = jnp.mean(x * x, axis=-1, keepdims=True)
    o_ref[0] = x * lax.rsqrt(ms + EPS) * g_ref[...]


def _final_norm_call(st, x, g):
    return pl.pallas_call(
        _final_norm_kernel,
        grid=(st.b, st.n_lat),
        in_specs=[st.row_spec(st.d), _vec_spec(st.d)],
        out_specs=st.row_spec(st.d),
        out_shape=jax.ShapeDtypeStruct((st.b, st.s, st.d), F32),
        compiler_params=_cp("parallel", "parallel"),
        name="final_norm",
    )(x, g.reshape(1, -1))


def _proj_res_kernel(a_ref, w_ref, bias_ref, x_ref, gate_ref, o_ref):
    y = jnp.dot(a_ref[0], w_ref[...], preferred_element_type=F32) + bias_ref[...]
    o_ref[0] = x_ref[0] + gate_ref[0] * y


def _proj_res_call(st, a, w, bias, x, gate):
    k, n = w.shape
    return pl.pallas_call(
        _proj_res_kernel,
        grid=(st.b, st.nt),
        in_specs=[st.row_spec(k), pl.BlockSpec((k, n), lambda b, i: (0, 0)), _vec_spec(n),
                  st.row_spec(n), st.mod_spec()],
        out_specs=st.row_spec(n),
        out_shape=jax.ShapeDtypeStruct((st.b, st.l, n), F32),
        compiler_params=_cp("parallel", "parallel"),
        name="proj_residual",
    )(a, w, bias.reshape(1, -1), x, gate)


def _proj_res_router_kernel(a_ref, w_ref, bias_ref, x_ref, gate_ref, g_ref, sc_ref, sh_ref, rw_ref,
                            o_ref, m_ref, aff_ref):
    y = jnp.dot(a_ref[0], w_ref[...], preferred_element_type=F32) + bias_ref[...]
    xn = x_ref[0] + gate_ref[0] * y
    o_ref[0] = xn
    m = _rms_mod(xn, g_ref[...], sc_ref[0], sh_ref[0]).astype(BF16)
    m_ref[0] = m
    logits = lax.dot_general(rw_ref[...], m, (((1,), (1,)), ((), ())), preferred_element_type=F32)
    e = jnp.exp(logits - jnp.max(logits, axis=0, keepdims=True))
    aff_ref[0] = e / jnp.sum(e, axis=0, keepdims=True)


def _proj_res_router_call(st, a, w, bias, x, gate, g, sc, sh, router_t):
    k, n = w.shape
    n_e = router_t.shape[0]
    return pl.pallas_call(
        _proj_res_router_kernel,
        grid=(st.b, st.nt),
        in_specs=[st.row_spec(k), pl.BlockSpec((k, n), lambda b, i: (0, 0)), _vec_spec(n),
                  st.row_spec(n), st.mod_spec(), _vec_spec(n), st.mod_spec(), st.mod_spec(),
                  pl.BlockSpec((n_e, n), lambda b, i: (0, 0))],
        out_specs=[st.row_spec(n), st.row_spec(n), pl.BlockSpec((1, n_e, st.tm), lambda b, i: (b, 0, i))],
        out_shape=[jax.ShapeDtypeStruct((st.b, st.l, n), F32), jax.ShapeDtypeStruct((st.b, st.l, n), BF16),
                   jax.ShapeDtypeStruct((st.b, n_e, st.l), F32)],
        compiler_params=_cp("parallel", "parallel"),
        name="proj_residual_router",
    )(a, w, bias.reshape(1, -1), x, gate, g.reshape(1, -1), sc, sh, router_t)


def _glu_kernel(a_ref, wa_ref, wb_ref, ba_ref, bb_ref, o_ref):
    a = a_ref[0]
    u = jnp.dot(a, wa_ref[...], preferred_element_type=F32) + ba_ref[...]
    v = jnp.dot(a, wb_ref[...], preferred_element_type=F32) + bb_ref[...]
    o_ref[0] = u * jax.nn.sigmoid(v)


def _glu_call(st, a, w, bias):
    k, n2 = w.shape
    n = n2 // 2
    tn = _divisor(n, 512, 128)
    nj = n // tn
    b2 = bias.reshape(1, -1)
    return pl.pallas_call(
        _glu_kernel,
        grid=(st.b, st.nt, nj),
        in_specs=[st.row_spec(k),
                  pl.BlockSpec((k, tn), lambda b, i, j: (0, j)),
                  pl.BlockSpec((k, tn), lambda b, i, j: (0, j + nj)),
                  pl.BlockSpec((1, tn), lambda b, i, j: (0, j)),
                  pl.BlockSpec((1, tn), lambda b, i, j: (0, j + nj))],
        out_specs=pl.BlockSpec((1, st.tm, tn), lambda b, i, j: (b, i, j)),
        out_shape=jax.ShapeDtypeStruct((st.b, st.l, n), F32),
        compiler_params=_cp("parallel", "parallel", "arbitrary"),
        name="conv_pw1_glu",
    )(a, w, w, b2, b2)


def _halo_specs(st):
    r = st.tm // HALO
    last = st.l // HALO - 1
    prev = pl.BlockSpec((1, HALO, st.d), lambda b, i: (b, jnp.maximum(i * r - 1, 0), 0))
    nxt = pl.BlockSpec((1, HALO, st.d), lambda b, i: (b, jnp.minimum((i + 1) * r, last), 0))
    return prev, nxt


def _segment_masks(st, i):
    is_ctx = i == st.n_lat
    seg_len = jnp.where(is_ctx, st.ctx, st.s)
    base = jnp.where(is_ctx, 0, i * st.tm)
    rows = lax.broadcasted_iota(jnp.int32, (st.tm, 1), 0)
    pos = base + rows
    valid = pos < seg_len
    prev_ok = jnp.logical_and(i != 0, jnp.logical_not(is_ctx))
    next_ok = i < st.n_lat - 1
    return valid, prev_ok, next_ok, pos, seg_len


def _dwconv_kernel(st, width, rc, cc, u_ref, up_ref, un_ref, w_ref, b_ref, g_ref, beta_ref, o_ref,
                   win_ref, sh_ref, acc_ref):
    i = pl.program_id(1)
    valid, prev_ok, next_ok, _, _ = _segment_masks(st, i)
    tm, d = st.tm, st.d
    pad = width // 2
    win_ref[0:HALO, :] = jnp.where(prev_ok, up_ref[0], 0.0)
    win_ref[HALO:HALO + tm, :] = jnp.where(valid, u_ref[0], 0.0)
    win_ref[HALO + tm:, :] = jnp.where(next_ok, un_ref[0], 0.0)
    n_sh = sh_ref.shape[1]

    for c0 in range(0, d, cc):
        for r in range(1, 8):
            sh_ref[r - 1] = win_ref[r:r + n_sh, c0:c0 + cc]

        def row_chunk(rb, carry, c0=c0):
            r0 = pl.multiple_of(rb * rc, rc)
            acc = jnp.zeros((rc, cc), F32)
            for k in range(width):
                off = HALO - pad + k
                a8, res = 8 * (off // 8), off % 8
                if res == 0:
                    tap = win_ref[pl.ds(r0 + a8, rc), c0:c0 + cc]
                else:
                    tap = sh_ref[res - 1, pl.ds(r0 + a8, rc), :]
                acc = acc + tap * w_ref[k:k + 1, c0:c0 + cc]
            acc_ref[pl.ds(r0, rc), c0:c0 + cc] = acc + b_ref[:, c0:c0 + cc]
            return carry

        lax.fori_loop(0, tm // rc, row_chunk, 0)
    y = acc_ref[...]
    mu = jnp.mean(y, axis=-1, keepdims=True)
    yc = y - mu
    var = jnp.mean(yc * yc, axis=-1, keepdims=True)
    z = yc * lax.rsqrt(var + EPS) * g_ref[...] + beta_ref[...]
    o_ref[0] = _silu(z).astype(o_ref.dtype)


def _dwconv_call(st, u, dw_w, dw_b, ln_g, ln_b):
    width = dw_w.shape[0]
    assert width // 2 < HALO
    prev, nxt = _halo_specs(st)
    rc = 32
    cc = _divisor(st.d, 256, 128)
    n_sh = st.tm + 2 * HALO - 8
    kern = functools.partial(_dwconv_kernel, st, width, rc, cc)
    return pl.pallas_call(
        kern,
        grid=(st.b, st.nt),
        in_specs=[st.row_spec(st.d), prev, nxt,
                  pl.BlockSpec((width, st.d), lambda b, i: (0, 0)),
                  _vec_spec(st.d), _vec_spec(st.d), _vec_spec(st.d)],
        out_specs=st.row_spec(st.d),
        out_shape=jax.ShapeDtypeStruct((st.b, st.l, st.d), BF16),
        scratch_shapes=[pltpu.VMEM((st.tm + 2 * HALO, st.d), F32), pltpu.VMEM((7, n_sh, cc), F32),
                        pltpu.VMEM((st.tm, st.d), F32)],
        compiler_params=_cp("parallel", "parallel"),
        name="dwconv_ln_silu",
    )(u, u, u, dw_w, dw_b.reshape(1, -1), ln_g.reshape(1, -1), ln_b.reshape(1, -1))


def _pool_kernel(st, x_ref, xp_ref, xn_ref, g_ref, sc_ref, sh_ref, w_ref, ps_ref, gate_ref, o_ref, win_ref):
    i = pl.program_id(1)
    valid, prev_ok, next_ok, pos, seg_len = _segment_masks(st, i)
    tm, d = st.tm, st.d
    g, sc, sh = g_ref[...], sc_ref[0], sh_ref[0]
    x = x_ref[0]
    n = jnp.where(valid, _rms_mod(x, g, sc, sh), 0.0)
    win_ref[0:HALO, :] = jnp.where(prev_ok, _rms_mod(xp_ref[0], g, sc, sh), 0.0)
    win_ref[HALO:HALO + tm, :] = n
    win_ref[HALO + tm:, :] = jnp.where(next_ok, _rms_mod(xn_ref[0], g, sc, sh), 0.0)
    grp = d // len(POOL_WINDOWS)
    for gi, w in enumerate(POOL_WINDOWS):
        c0 = gi * grp
        tot = jnp.zeros((tm, grp), F32)
        for j in range(-(w // 2), w - w // 2):
            tot = tot + win_ref[HALO + j:HALO + j + tm, c0:c0 + grp]
        cnt = jnp.minimum(pos - w // 2 + w, seg_len) - jnp.maximum(pos - w // 2, 0)
        cnt = jnp.maximum(cnt, 1).astype(F32)
        diff = (tot / cnt - n[:, c0:c0 + grp]).astype(BF16)
        y = jnp.dot(diff, w_ref[gi], preferred_element_type=F32) * ps_ref[:, c0:c0 + grp]
        o_ref[0, :, c0:c0 + grp] = x[:, c0:c0 + grp] + gate_ref[0][:, c0:c0 + grp] * y


def _pool_call(st, x, g, sc, sh, pool_w, pool_scale, gate):
    prev, nxt = _halo_specs(st)
    ng, grp, _ = pool_w.shape
    return pl.pallas_call(
        functools.partial(_pool_kernel, st),
        grid=(st.b, st.nt),
        in_specs=[st.row_spec(st.d), prev, nxt, _vec_spec(st.d), st.mod_spec(), st.mod_spec(),
                  pl.BlockSpec((ng, grp, grp), lambda b, i: (0, 0, 0)), _vec_spec(st.d), st.mod_spec()],
        out_specs=st.row_spec(st.d),
        out_shape=jax.ShapeDtypeStruct((st.b, st.l, st.d), F32),
        scratch_shapes=[pltpu.VMEM((st.tm + 2 * HALO, st.d), F32)],
        compiler_params=_cp("parallel", "parallel"),
        name="pool_mixer",
    )(x, x, x, g.reshape(1, -1), sc, sh, pool_w, pool_scale.reshape(1, -1), gate)


def _rope_tables(st, rot_dim):
    quarter = rot_dim // 4
    rows = st.s // GRID_W
    inv = ROPE_THETA ** (-jnp.arange(quarter, dtype=F32) / quarter)
    row_ang = jnp.arange(rows, dtype=F32)[:, None, None] * inv
    col_ang = jnp.arange(GRID_W, dtype=F32)[None, :, None] * inv
    ang = jnp.concatenate([jnp.broadcast_to(row_ang, (rows, GRID_W, quarter)),
                           jnp.broadcast_to(col_ang, (rows, GRID_W, quarter))], axis=-1).reshape(st.s, 2 * quarter)
    cos, sin = jnp.cos(ang), jnp.sin(ang)
    c_tab = jnp.concatenate([cos, cos], axis=-1)
    s_tab = jnp.concatenate([-sin, sin], axis=-1)
    c_tab = jnp.concatenate([c_tab, jnp.ones((st.ctx, rot_dim), F32)], axis=0)
    s_tab = jnp.concatenate([s_tab, jnp.zeros((st.ctx, rot_dim), F32)], axis=0)
    return c_tab, s_tab


def _tab_spec(st, width):
    return pl.BlockSpec((st.tm, width), lambda b, i, *_: (i, 0))


def _folded_rope(r, cs):
    t = r * cs
    t = t + pltpu.roll(t, 64, axis=1)
    lane = lax.broadcasted_iota(jnp.int32, t.shape, 1)
    return jnp.where(lane < MLA_ROPE, t, 0.0)


def _mla_down_kernel(q_rank, kv_rank, n_ref, w_ref, qn_ref, kvn_ref, cs_ref, cq_ref, ckv_ref, kr_ref):
    acc = jnp.dot(n_ref[0], w_ref[...], preferred_element_type=F32)
    cq = acc[:, :q_rank]
    cq_ref[0] = (cq * lax.rsqrt(jnp.mean(cq * cq, axis=-1, keepdims=True) + EPS) * qn_ref[...]).astype(BF16)
    ckv = acc[:, q_rank:q_rank + kv_rank]
    ckv_ref[0] = (ckv * lax.rsqrt(jnp.mean(ckv * ckv, axis=-1, keepdims=True) + EPS) * kvn_ref[...]).astype(BF16)
    kr_ref[0] = _folded_rope(acc[:, q_rank + kv_rank:], cs_ref[...]).astype(BF16)


def _mla_down_call(st, n, w_cat, q_norm, kv_norm, cs_tab):
    k, ncols = w_cat.shape
    q_rank, kv_rank = q_norm.shape[0], kv_norm.shape[0]
    assert ncols == q_rank + kv_rank + 128 and q_rank % 128 == 0 and kv_rank % 128 == 0
    return pl.pallas_call(
        functools.partial(_mla_down_kernel, q_rank, kv_rank),
        grid=(st.b, st.nt),
        in_specs=[st.row_spec(k), pl.BlockSpec((k, ncols), lambda b, i: (0, 0)),
                  _vec_spec(q_rank), _vec_spec(kv_rank), _tab_spec(st, 128)],
        out_specs=[st.row_spec(q_rank), st.row_spec(kv_rank), st.row_spec(128)],
        out_shape=[jax.ShapeDtypeStruct((st.b, st.l, q_rank), BF16),
                   jax.ShapeDtypeStruct((st.b, st.l, kv_rank), BF16),
                   jax.ShapeDtypeStruct((st.b, st.l, 128), BF16)],
        compiler_params=_cp("parallel", "parallel"),
        name="mla_down",
    )(n, w_cat, q_norm.reshape(1, -1), kv_norm.reshape(1, -1), cs_tab)


def _mla_uq_kernel(heads, qscale, cq_ref, w_ref, cs_ref, qt_ref):
    cq, cs = cq_ref[0], cs_ref[...]
    for h in range(heads):
        acc = jnp.dot(cq, w_ref[:, h * MLA_QK_PAD:(h + 1) * MLA_QK_PAD], preferred_element_type=F32)
        q = jnp.concatenate([acc[:, :MLA_NOPE], _folded_rope(acc[:, MLA_NOPE:], cs)], axis=1) * qscale
        qt_ref[0, h] = q.T.astype(BF16)


def _mla_uq_call(st, cq, w_ext, cs_tab, heads, qscale):
    k = cq.shape[-1]
    return pl.pallas_call(
        functools.partial(_mla_uq_kernel, heads, qscale),
        grid=(st.b, st.nt),
        in_specs=[st.row_spec(k), pl.BlockSpec((k, heads * MLA_QK_PAD), lambda b, i: (0, 0)), _tab_spec(st, 128)],
        out_specs=pl.BlockSpec((1, heads, MLA_QK_PAD, st.tm), lambda b, i: (b, 0, 0, i)),
        out_shape=jax.ShapeDtypeStruct((st.b, heads, MLA_QK_PAD, st.l), BF16),
        compiler_params=_cp("parallel", "parallel"),
        name="mla_up_q",
    )(cq, w_ext, cs_tab)


def _mla_ukv_kernel(heads, ckv_ref, w_ref, kr_ref, k_ref, vt_ref):
    ckv, kr = ckv_ref[0], kr_ref[0]
    hw = MLA_NOPE + MLA_V
    for h in range(heads):
        acc = jnp.dot(ckv, w_ref[:, h * hw:(h + 1) * hw], preferred_element_type=F32)
        k_ref[0, h, :, :MLA_NOPE] = acc[:, :MLA_NOPE].astype(BF16)
        k_ref[0, h, :, MLA_NOPE:] = kr
        vt_ref[0, h] = acc[:, MLA_NOPE:].T.astype(BF16)


def _mla_ukv_call(st, ckv, w_ukv, kr, heads):
    k = ckv.shape[-1]
    hw = MLA_NOPE + MLA_V
    return pl.pallas_call(
        functools.partial(_mla_ukv_kernel, heads),
        grid=(st.b, st.nt),
        in_specs=[st.row_spec(k), pl.BlockSpec((k, heads * hw), lambda b, i: (0, 0)), st.row_spec(128)],
        out_specs=[pl.BlockSpec((1, heads, st.tm, MLA_QK_PAD), lambda b, i: (b, 0, i, 0)),
                   pl.BlockSpec((1, heads, MLA_V, st.tm), lambda b, i: (b, 0, 0, i))],
        out_shape=[jax.ShapeDtypeStruct((st.b, heads, st.l, MLA_QK_PAD), BF16),
                   jax.ShapeDtypeStruct((st.b, heads, MLA_V, st.l), BF16)],
        compiler_params=_cp("parallel", "parallel"),
        name="mla_up_kv",
    )(ckv, w_ukv, kr)


def _softmax_chunk_t(st, tk, s_ref, j, vt, m_ref, l_ref, acc_ref, masked):
    s = s_ref[...]
    if masked:
        row = j * tk + lax.broadcasted_iota(jnp.int32, s.shape, 0)
        s = jnp.where(row >= st.s, s, NEG_BIG)
    m_prev = m_ref[...]
    m_new = jnp.maximum(m_prev, jnp.max(s, axis=0, keepdims=True))
    alpha = jnp.exp2(m_prev - m_new)
    p = jnp.exp2(s - m_new)
    l_ref[...] = alpha * l_ref[...] + jnp.sum(p, axis=0, keepdims=True)
    acc_ref[...] = alpha * acc_ref[...] + jnp.dot(vt, p.astype(BF16), preferred_element_type=F32)
    m_ref[...] = m_new


def _pipelined_key_loop(st, tq, tk, qi, qk, qk_next, proc, bufs0, bufs1, bufs2):
    n_lat_q = st.s // tq
    nch = st.l // tk
    carry_first_chunk = nch % 2 == 1 and nch >= 3

    def pair(jj, carry):
        qk(2 * jj + 1, bufs1)
        proc(2 * jj, bufs0, False)
        qk(2 * jj + 2, bufs0)
        proc(2 * jj + 1, bufs1, False)
        return carry

    if carry_first_chunk:
        @pl.when(qi == 0)
        def _():
            qk(0, bufs2)

        @pl.when(qi < n_lat_q)
        def _():
            qk(1, bufs1)
            proc(0, bufs2, False)
            qk(2, bufs0)
            proc(1, bufs1, False)
            lax.fori_loop(1, (nch - 1) // 2, pair, 0)
            qk_next(bufs2)
            proc(nch - 1, bufs0, False)
    else:
        @pl.when(qi < n_lat_q)
        def _():
            qk(0, bufs0)
            lax.fori_loop(0, (nch - 1) // 2, pair, 0)
            if nch % 2 == 1:
                proc(nch - 1, bufs0, False)
            else:
                qk(nch - 1, bufs1)
                proc(nch - 2, bufs0, False)
                proc(nch - 1, bufs1, False)

    @pl.when(qi == n_lat_q)
    def _():
        for j in range(st.s // tk, nch):
            qk(j, bufs0)
            proc(j, bufs0, True)


def _chunk(j, tk):
    return pl.ds(pl.multiple_of(j * tk, tk), tk)


def _mla_attn_kernel(st, tq, tk, qt_ref, qt_next_ref, k_ref, vt_ref, o_ref, s0_ref, s1_ref, s2_ref,
                     m_ref, l_ref, acc_ref):
    m_ref[...] = jnp.full_like(m_ref, NEG_BIG)
    l_ref[...] = jnp.zeros_like(l_ref)
    acc_ref[...] = jnp.zeros_like(acc_ref)

    def scores(j, q_ref, s_ref):
        s_ref[...] = jnp.dot(k_ref[0, 0, _chunk(j, tk), :], q_ref[0, 0], preferred_element_type=F32)

    def proc(j, s_ref, masked):
        _softmax_chunk_t(st, tk, s_ref, j, vt_ref[0, 0, :, _chunk(j, tk)], m_ref, l_ref, acc_ref, masked)

    _pipelined_key_loop(st, tq, tk, pl.program_id(2),
                        lambda j, s_ref: scores(j, qt_ref, s_ref), lambda s_ref: scores(0, qt_next_ref, s_ref),
                        proc, s0_ref, s1_ref, s2_ref)
    o_ref[0] = (acc_ref[...] / l_ref[...]).T.astype(o_ref.dtype)


def _attn_tiles(st, tq_target):
    tq = _divisor(st.s, tq_target, 128)
    assert st.ctx <= tq
    tk = _divisor(st.l, 1536, 256 if st.l % 256 == 0 else 128)
    return tq, tk


def _mla_attn_call(st, qt, k, vt, heads):
    tq, tk = _attn_tiles(st, 1024)
    last = st.s // tq
    return pl.pallas_call(
        functools.partial(_mla_attn_kernel, st, tq, tk),
        grid=(st.b, heads, last + 1),
        in_specs=[pl.BlockSpec((1, 1, MLA_QK_PAD, tq), lambda b, h, i: (b, h, 0, i)),
                  pl.BlockSpec((1, 1, MLA_QK_PAD, tq), lambda b, h, i: (b, h, 0, jnp.minimum(i + 1, last))),
                  pl.BlockSpec((1, 1, st.l, MLA_QK_PAD), lambda b, h, i: (b, h, 0, 0)),
                  pl.BlockSpec((1, 1, MLA_V, st.l), lambda b, h, i: (b, h, 0, 0))],
        out_specs=pl.BlockSpec((1, tq, MLA_V), lambda b, h, i: (b, i, h)),
        out_shape=jax.ShapeDtypeStruct((st.b, st.l, heads * MLA_V), BF16),
        scratch_shapes=[pltpu.VMEM((tk, tq), F32)] * 3
        + [pltpu.VMEM((1, tq), F32), pltpu.VMEM((1, tq), F32), pltpu.VMEM((MLA_V, tq), F32)],
        compiler_params=_cp("arbitrary", "arbitrary", "arbitrary"),
        name="mla_attention",
    )(qt, qt, k, vt)


def _diff_proj_kernel(heads, mode, qscale, n_ref, w_ref, c_ref, s_ref, o_ref):
    hd = DIFF_HEAD_DIM
    acc = jnp.dot(n_ref[0], w_ref[...], preferred_element_type=F32)
    if mode != "v":
        c, s = c_ref[...], s_ref[...]
    for h in range(heads):
        if mode == "v":
            o_ref[0, h] = acc[:, 2 * h * hd:2 * (h + 1) * hd].T.astype(BF16)
            continue
        halves = []
        for n in range(2):
            xh = acc[:, (2 * h + n) * hd:(2 * h + n + 1) * hd]
            halves.append(xh * c + pltpu.roll(xh, hd // 2, axis=1) * s)
        if mode == "k":
            o_ref[0, :, 2 * h * hd:(2 * h + 1) * hd] = halves[0].astype(BF16)
            o_ref[0, :, (2 * h + 1) * hd:2 * (h + 1) * hd] = halves[1].astype(BF16)
        else:
            o_ref[0, h] = (jnp.concatenate(halves, axis=1) * qscale).T.astype(BF16)


def _diff_proj_call(st, n, w_qkv, c_tab, s_tab, heads, mode, qscale=1.0):
    k = w_qkv.shape[0]
    hw = 2 * DIFF_HEAD_DIM
    third = {"q": 0, "k": 1, "v": 2}[mode]
    if mode == "k":
        out_spec = st.row_spec(heads * hw)
        out_shape = jax.ShapeDtypeStruct((st.b, st.l, heads * hw), BF16)
    else:
        out_spec = pl.BlockSpec((1, heads, hw, st.tm), lambda b, i: (b, 0, 0, i))
        out_shape = jax.ShapeDtypeStruct((st.b, heads, hw, st.l), BF16)
    return pl.pallas_call(
        functools.partial(_diff_proj_kernel, heads, mode, qscale),
        grid=(st.b, st.nt),
        in_specs=[st.row_spec(k), pl.BlockSpec((k, heads * hw), lambda b, i: (0, third)),
                  _tab_spec(st, DIFF_HEAD_DIM), _tab_spec(st, DIFF_HEAD_DIM)],
        out_specs=out_spec,
        out_shape=out_shape,
        compiler_params=_cp("parallel", "parallel"),
        name="diff_proj_" + mode,
    )(n, w_qkv, c_tab, s_tab)


def _diff_attn_kernel(st, tq, tk, lam_init, qt_ref, qt_next_ref, k_ref, vt_ref, lq1_ref, lk1_ref, lq2_ref, lk2_ref,
                      sg_ref, o_ref, sa0, sb0, sa1, sb1, sa2, sb2, m1, l1, a1, m2, l2, a2):
    hd = DIFF_HEAD_DIM
    for m_ref, l_ref, a_ref in ((m1, l1, a1), (m2, l2, a2)):
        m_ref[...] = jnp.full_like(m_ref, NEG_BIG)
        l_ref[...] = jnp.zeros_like(l_ref)
        a_ref[...] = jnp.zeros_like(a_ref)

    def scores(j, q_ref, bufs):
        kc = k_ref[0, _chunk(j, tk), :]
        bufs[0][...] = jnp.dot(kc[:, :hd], q_ref[0, 0, :hd, :], preferred_element_type=F32)
        bufs[1][...] = jnp.dot(kc[:, hd:], q_ref[0, 0, hd:, :], preferred_element_type=F32)

    def proc(j, bufs, masked):
        vt = vt_ref[0, 0, :, _chunk(j, tk)]
        _softmax_chunk_t(st, tk, bufs[0], j, vt, m1, l1, a1, masked)
        _softmax_chunk_t(st, tk, bufs[1], j, vt, m2, l2, a2, masked)

    _pipelined_key_loop(st, tq, tk, pl.program_id(2),
                        lambda j, bufs: scores(j, qt_ref, bufs), lambda bufs: scores(0, qt_next_ref, bufs),
                        proc, (sa0, sb0), (sa1, sb1), (sa2, sb2))
    lam = (jnp.exp(jnp.sum(lq1_ref[...] * lk1_ref[...], axis=-1, keepdims=True))
           - jnp.exp(jnp.sum(lq2_ref[...] * lk2_ref[...], axis=-1, keepdims=True)) + lam_init)
    o = a1[...] / l1[...] - lam * (a2[...] / l2[...])
    o = o * lax.rsqrt(jnp.mean(o * o, axis=0, keepdims=True) + EPS) * sg_ref[...]
    o_ref[0] = (o * (1.0 - lam_init)).T.astype(o_ref.dtype)


def _diff_attn_call(st, qt, k, vt, lq1, lk1, lq2, lk2, subln_g, heads, lam_init):
    tq, tk = _attn_tiles(st, 512)
    hw = 2 * DIFF_HEAD_DIM
    vec = lambda a: a.reshape(1, -1)
    last = st.s // tq
    return pl.pallas_call(
        functools.partial(_diff_attn_kernel, st, tq, tk, lam_init),
        grid=(st.b, heads, last + 1),
        in_specs=[pl.BlockSpec((1, 1, hw, tq), lambda b, h, i: (b, h, 0, i)),
                  pl.BlockSpec((1, 1, hw, tq), lambda b, h, i: (b, h, 0, jnp.minimum(i + 1, last))),
                  pl.BlockSpec((1, st.l, hw), lambda b, h, i: (b, 0, h), pipeline_mode=pl.Buffered(1)),
                  pl.BlockSpec((1, 1, hw, st.l), lambda b, h, i: (b, h, 0, 0), pipeline_mode=pl.Buffered(1)),
                  _vec_spec(DIFF_HEAD_DIM), _vec_spec(DIFF_HEAD_DIM), _vec_spec(DIFF_HEAD_DIM),
                  _vec_spec(DIFF_HEAD_DIM), pl.BlockSpec((hw, 1), lambda *_: (0, 0))],
        out_specs=pl.BlockSpec((1, tq, hw), lambda b, h, i: (b, i, h)),
        out_shape=jax.ShapeDtypeStruct((st.b, st.l, heads * hw), BF16),
        scratch_shapes=[pltpu.VMEM((tk, tq), F32)] * 6
        + [pltpu.VMEM((1, tq), F32), pltpu.VMEM((1, tq), F32), pltpu.VMEM((hw, tq), F32)] * 2,
        compiler_params=_cp("arbitrary", "arbitrary", "arbitrary"),
        name="diff_attention",
    )(qt, qt, k, vt, vec(lq1), vec(lk1), vec(lq2), vec(lk2), subln_g.reshape(-1, 1))


W_CHUNKS = 16


def _expert_kernel(tm, n_local, nt, layer, e0, idx_hbm, x_ref, wg_hbm, wu_hbm, wd_hbm, g_ref, gate_ref, acc_in,
                   acc_hbm, idx_smem, rows_ref, wg_bf, wu_bf, wd_bf, stage_g, stage_u, stage_d,
                   sem_idx, sem_gather, sem_scatter, sem_w):
    del acc_in
    e_local, tile = pl.program_id(0), pl.program_id(1)
    n_steps = n_local * nt
    first_of_expert = tile == 0
    k = e_local * nt + tile
    slot = lax.rem(k, 2)
    wslot = lax.rem(e_local, 2)
    rows_g, rows_d = wg_bf.shape[1] // W_CHUNKS, wd_bf.shape[1] // W_CHUNKS

    def w_copies(expert, c, s):
        cg, cd = pl.ds(pl.multiple_of(c * rows_g, rows_g), rows_g), pl.ds(pl.multiple_of(c * rows_d, rows_d), rows_d)
        return (pltpu.make_async_copy(wg_hbm.at[layer, expert, cg, :], stage_g.at[s], sem_w.at[s]),
                pltpu.make_async_copy(wu_hbm.at[layer, expert, cg, :], stage_u.at[s], sem_w.at[s]),
                pltpu.make_async_copy(wd_hbm.at[layer, expert, cd, :], stage_d.at[s], sem_w.at[s]))

    def w_start(expert, c, s):
        for cp in w_copies(expert, c, s):
            cp.start()

    def w_finish(expert, c, s, dst):
        for cp in w_copies(expert, c, s):
            cp.wait()
        cg, cd = pl.ds(pl.multiple_of(c * rows_g, rows_g), rows_g), pl.ds(pl.multiple_of(c * rows_d, rows_d), rows_d)
        wg_bf[dst, cg, :] = stage_g[s].astype(BF16)
        wu_bf[dst, cg, :] = stage_u[s].astype(BF16)
        wd_bf[dst, cd, :] = stage_d[s].astype(BF16)

    def w_stream(expert, chunks, dst, first_started):
        slots = stage_g.shape[0]
        for n in range(1 if first_started else 0, min(slots, len(chunks))):
            w_start(expert, chunks[n], n)
        for n, c in enumerate(chunks):
            w_finish(expert, c, n % slots, dst)
            if n + slots < len(chunks):
                w_start(expert, chunks[n + slots], n % slots)

    @pl.when(k == 0)
    def _():
        w_stream(e0, list(range(W_CHUNKS)), 0, False)

    per_tile = W_CHUNKS // nt
    next_chunks = [tile * per_tile + n for n in range(per_tile)]
    has_next = e_local + 1 < n_local

    @pl.when(has_next)
    def _():
        w_start(e0 + e_local + 1, next_chunks[0], 0)

    def idx_copy(step, s):
        return pltpu.make_async_copy(idx_hbm.at[step], idx_smem.at[s], sem_idx.at[s])

    def row_copy(s, r, sem, to_hbm):
        hbm, vmem = acc_hbm.at[pl.ds(idx_smem[s, r], 1), :], rows_ref.at[s, pl.ds(r, 1), :]
        return pltpu.make_async_copy(vmem, hbm, sem) if to_hbm else pltpu.make_async_copy(hbm, vmem, sem)

    def for_this_slot(body):
        for s in (0, 1):
            pl.when(slot == s)(functools.partial(body, s))

    def all_rows(sem):
        return pltpu.make_async_copy(acc_hbm.at[pl.ds(0, tm), :], rows_ref.at[0], sem)

    @pl.when(k == 0)
    def _():
        idx_copy(0, 0).start()

    idx_copy(k, slot).wait()

    @pl.when(k + 1 < n_steps)
    def _():
        idx_copy(k + 1, 1 - slot).start()

    @pl.when(jnp.logical_and(k > 0, first_of_expert))
    def _():
        all_rows(sem_scatter).wait()

    def gather(s):
        for r in range(tm):
            row_copy(s, r, sem_gather, False).start()

    for_this_slot(gather)

    x = x_ref[0]
    a = jnp.dot(x, wg_bf[wslot], preferred_element_type=F32)
    u = jnp.dot(x, wu_bf[wslot], preferred_element_type=F32)
    h = (_silu(a) * u).astype(BF16)
    y = jnp.dot(h, wd_bf[wslot], preferred_element_type=F32) * g_ref[0]

    @pl.when(has_next)
    def _():
        w_stream(e0 + e_local + 1, next_chunks, 1 - wslot, True)

    all_rows(sem_gather).wait()

    @pl.when(jnp.logical_not(first_of_expert))
    def _():
        all_rows(sem_scatter).wait()

    def accumulate_and_write_back(s):
        rows_ref[s] = rows_ref[s] + gate_ref[0] * y
        for r in range(tm):
            row_copy(s, r, sem_scatter, True).start()

    for_this_slot(accumulate_and_write_back)

    @pl.when(k == n_steps - 1)
    def _():
        all_rows(sem_scatter).wait()


def _expert_call(xs, wg, wu, wd, layer, gates, idx, acc, gate, tm, gate_row0, e0):
    n_e, rows, d = xs.shape
    ff = wg.shape[-1]
    nt = rows // tm
    assert W_CHUNKS % nt == 0 and d % (16 * W_CHUNKS) == 0 and ff % (16 * W_CHUNKS) == 0
    tiles_per_sample = nt // (gate.shape[0] // 2)
    slots = max(2, W_CHUNKS // nt)
    params = pltpu.CompilerParams(dimension_semantics=("arbitrary", "arbitrary"), vmem_limit_bytes=VMEM_LIMIT_BYTES,
                                  disable_bounds_checks=True)
    hbm = pl.BlockSpec(memory_space=pl.ANY)
    return pl.pallas_call(
        functools.partial(_expert_kernel, tm, n_e, nt, layer, e0),
        grid=(n_e, nt),
        in_specs=[hbm,
                  pl.BlockSpec((1, tm, d), lambda e, i: (e, i, 0)),
                  hbm, hbm, hbm,
                  pl.BlockSpec((1, tm, 1), lambda e, i: (e, i, 0)),
                  pl.BlockSpec((1, 1, d), lambda e, i: ((i // tiles_per_sample) * 2 + gate_row0, 0, 0)),
                  hbm],
        out_specs=hbm,
        out_shape=jax.ShapeDtypeStruct(acc.shape, F32),
        input_output_aliases={7: 0},
        scratch_shapes=[pltpu.SMEM((2, tm), jnp.int32), pltpu.VMEM((2, tm, d), F32),
                        pltpu.VMEM((2, d, ff), BF16), pltpu.VMEM((2, d, ff), BF16), pltpu.VMEM((2, ff, d), BF16),
                        pltpu.VMEM((slots, d // W_CHUNKS, ff), F32), pltpu.VMEM((slots, d // W_CHUNKS, ff), F32),
                        pltpu.VMEM((slots, ff // W_CHUNKS, d), F32),
                        pltpu.SemaphoreType.DMA((2,)), pltpu.SemaphoreType.DMA, pltpu.SemaphoreType.DMA,
                        pltpu.SemaphoreType.DMA((slots,))],
        compiler_params=params,
        name="moe_experts",
    )(idx.reshape(n_e * nt, tm), xs, wg, wu, wd, gates, gate, acc)


def _routed_experts(st, m, aff_seg, offset, cap, n_groups, acc, wg, wu, wd, layer, gate, gate_row0):
    n_e = aff_seg.shape[1]
    g_sel, idx = lax.top_k(aff_seg, cap)
    idx = idx + offset
    bidx = jnp.arange(st.b)[:, None, None]
    rows = (idx + bidx * st.l).astype(jnp.int32)
    per = n_e // n_groups
    tm = _divisor(cap, 256, 16)
    for e0 in range(0, n_e, per):
        sl = slice(e0, e0 + per)
        xs = jnp.swapaxes(m[bidx, idx[:, sl]], 0, 1).reshape(per, st.b * cap, st.d)
        gates = jnp.swapaxes(g_sel[:, sl], 0, 1).reshape(per, st.b * cap, 1)
        rows_g = jnp.swapaxes(rows[:, sl], 0, 1).reshape(per, st.b * cap)
        acc = _expert_call(xs, wg, wu, wd, layer, gates, rows_g, acc, gate, tm, gate_row0, e0)
    return acc


def _moe(st, x, g, sc, sh, gate, router_w, wg, wu, wd, layer, with_ctx, routed=None):
    n_e = router_w.shape[-1]
    m, aff = routed if routed is not None else _norm_router_call(st, x, g, sc, sh, router_w.T.astype(BF16))
    acc = x.reshape(st.b * st.l, st.d)
    cap_l = max(1, EC_FACTOR * st.s // n_e)
    acc = _routed_experts(st, m, aff[:, :, :st.s], 0, cap_l, 2 if n_e % 2 == 0 else 1, acc, wg, wu, wd, layer,
                          gate, 0)
    if with_ctx:
        cap_c = max(1, EC_FACTOR * st.ctx // n_e)
        acc = _routed_experts(st, m, aff[:, :, st.s:], st.s, cap_c, 1, acc, wg, wu, wd, layer, gate, 1)
    return acc.reshape(st.b, st.l, st.d)


def _swap_halves_cols(w, width):
    k, n = w.shape
    w = w.reshape(k, n // width, 2, width // 2)
    return w[:, :, ::-1, :].reshape(k, n)


def kernel(x, c, ctx, c_ctx, ada_w, ada_b, norm_g, final_g, conv_pw1_w, conv_pw1_b, conv_dw_w, conv_dw_b, conv_ln_g, conv_ln_b, conv_pw2_w, conv_pw2_b, mla_w_dq, mla_q_norm, mla_w_uq, mla_w_dkv, mla_kv_norm, mla_w_ukv, mla_w_o, pool_w, pool_scale, diff_w_qkv, diff_lq1, diff_lk1, diff_lq2, diff_lk2, diff_subln_g, diff_w_o, moe_router, moe_w_gate, moe_w_up, moe_w_down):
    b, s, d = x.shape
    n_ctx = ctx.shape[1]
    depth = ada_w.shape[0]
    st = _Stream(b, s, n_ctx, d, _divisor(s, 512, max(n_ctx, 128)))
    xs = jnp.concatenate([x, ctx], axis=1)

    c8 = jnp.zeros((8, d), F32).at[:b].set(c).at[b].set(c_ctx)
    mod = _mod_call(c8, ada_w, ada_b)
    order = jnp.stack([jnp.arange(b), jnp.full((b,), b)], axis=1).reshape(-1)
    mod = mod[:, order, :].reshape(depth, 2 * b, 6, d)
    mods = [[mod[i, :, k, :].reshape(2 * b, 1, d) for k in range(6)] for i in range(depth)]

    heads = mla_w_uq.shape[1] // (MLA_NOPE + MLA_ROPE)
    q_rank, kv_rank = mla_q_norm.shape[0], mla_kv_norm.shape[0]
    diff_heads = diff_w_o.shape[0] // (2 * DIFF_HEAD_DIM)
    zeros_d = jnp.zeros((d,), F32)

    for i in range(depth):
        kind = i % 4
        need_ctx_out = i < depth - 1
        sh1, sc1, gt1, sh2, sc2, gt2 = mods[i]
        routed = None
        router_t = moe_router[i].T.astype(BF16)

        def out_proj(a, w, bias, xs=xs, i=i, gt1=gt1, sc2=sc2, sh2=sh2, router_t=router_t):
            xn, m, aff = _proj_res_router_call(st, a, w.astype(BF16), bias, xs, gt1, norm_g[i, 1], sc2, sh2, router_t)
            return xn, (m, aff)

        if kind == 0:
            n = _norm_call(st, xs, norm_g[i, 0], sc1, sh1)
            u = _glu_call(st, n, conv_pw1_w.astype(BF16), conv_pw1_b)
            v = _dwconv_call(st, u, conv_dw_w, conv_dw_b, conv_ln_g, conv_ln_b)
            xs, routed = out_proj(v, conv_pw2_w, conv_pw2_b)
        elif kind == 1:
            n = _norm_call(st, xs, norm_g[i, 0], sc1, sh1)
            c_tab, s_tab = _rope_tables(st, MLA_ROPE)
            cs_tab = jnp.concatenate([c_tab, s_tab], axis=-1)
            w_kr = mla_w_dkv[:, kv_rank:]
            w_cat = jnp.concatenate([mla_w_dq, mla_w_dkv[:, :kv_rank], w_kr, _swap_halves_cols(w_kr, MLA_ROPE)],
                                    axis=1).astype(BF16)
            cq, ckv, kr = _mla_down_call(st, n, w_cat, mla_q_norm, mla_kv_norm, cs_tab)
            w_uq = mla_w_uq.reshape(q_rank, heads, MLA_NOPE + MLA_ROPE)
            w_uq_rope = w_uq[:, :, MLA_NOPE:].reshape(q_rank, heads * MLA_ROPE)
            w_uq_ext = jnp.concatenate(
                [w_uq, _swap_halves_cols(w_uq_rope, MLA_ROPE).reshape(q_rank, heads, MLA_ROPE)], axis=-1)
            qscale = (MLA_NOPE + MLA_ROPE) ** -0.5 * LOG2E
            qt = _mla_uq_call(st, cq, w_uq_ext.reshape(q_rank, heads * MLA_QK_PAD).astype(BF16), cs_tab, heads, qscale)
            k, vt = _mla_ukv_call(st, ckv, mla_w_ukv.astype(BF16), kr, heads)
            o = _mla_attn_call(st, qt, k, vt, heads)
            xs, routed = out_proj(o, mla_w_o, zeros_d)
        elif kind == 2:
            xs = _pool_call(st, xs, norm_g[i, 0], sc1, sh1, pool_w.astype(BF16), pool_scale, gt1)
        else:
            n = _norm_call(st, xs, norm_g[i, 0], sc1, sh1)
            c_tab, s_tab = _rope_tables(st, DIFF_HEAD_DIM)
            w_qkv = diff_w_qkv.astype(BF16)
            qt = _diff_proj_call(st, n, w_qkv, c_tab, s_tab, diff_heads, "q", DIFF_HEAD_DIM ** -0.5 * LOG2E)
            k = _diff_proj_call(st, n, w_qkv, c_tab, s_tab, diff_heads, "k")
            vt = _diff_proj_call(st, n, w_qkv, c_tab, s_tab, diff_heads, "v")
            lam_init = 0.8 - 0.6 * math.exp(-0.3 * i)
            o = _diff_attn_call(st, qt, k, vt, diff_lq1, diff_lk1, diff_lq2, diff_lk2, diff_subln_g, diff_heads,
                                lam_init)
            xs, routed = out_proj(o, diff_w_o, zeros_d)
        xs = _moe(st, xs, norm_g[i, 1], sc2, sh2, gt2, moe_router[i], moe_w_gate, moe_w_up, moe_w_down, i,
                  need_ctx_out, routed)
    return _final_norm_call(st, xs, final_g)
```
